```python
import jax
import jax.numpy as jnp
from jax import lax
import numpy as np

D_MODEL = 1024
BATCH = 4
SEQ = 4096
DEPTH = 2
DEC_BATCH = 32
DEC_SEQ = 8
PAST_LEN = 8192
PAGE_SIZE = 128

N_MIXERS = 2
N_A = (DEPTH + 1) // 2
N_B = DEPTH // 2
MIX_WIDTH = 3 * D_MODEL // 4
MEM_WIDTH = D_MODEL - MIX_WIDTH
CHUNK = 128
GM_GROUP_DIM = 128
GM_GROUPS = MIX_WIDTH // GM_GROUP_DIM
SB_HEAD_DIM = 64
SB_HEADS = MIX_WIDTH // SB_HEAD_DIM
SB_BLOCK = 128
SB_BIAS_INIT = -5.0
MEM_TOKENS = 256
MEM_HEADS = 4
MEM_HEAD_DIM = MEM_WIDTH // MEM_HEADS
D_FF = 128 * ((8 * D_MODEL // 3 + 127) // 128)
CONV_W = 3
EPS = 1e-6

kernel_name = 'hybrid_gmlp_stickbreak_memxattn_convffn_step'


def rms_norm(x, gain):
    x32 = x.astype(jnp.float32)
    y = x32 * lax.rsqrt(jnp.mean(x32 * x32, axis=-1, keepdims=True) + EPS)
    return (y * gain.astype(jnp.float32)).astype(x.dtype)


def gmlp_mixer(h, w_in, v_gain, ws, bs):
    B, T, _ = h.shape
    proj = h @ w_in
    u = jax.nn.gelu(proj[..., :MIX_WIDTH])
    v = rms_norm(jax.nn.gelu(proj[..., MIX_WIDTH:2 * MIX_WIDTH]), v_gain)
    mq = proj[..., 2 * MIX_WIDTH:].reshape(B, T, MEM_HEADS, MEM_HEAD_DIM)
    n_chunks = -(-T // CHUNK)
    vc = jnp.pad(v, ((0, 0), (0, n_chunks * CHUNK - T), (0, 0)))
    vc = vc.reshape(B, n_chunks, CHUNK, GM_GROUPS, GM_GROUP_DIM)
    causal = jnp.tril(jnp.ones((CHUNK, CHUNK), dtype=bool))
    w = jnp.where(causal, ws, jnp.zeros_like(ws))
    mixed = jnp.einsum('gts,bcsge->bctge', w, vc) + bs.T[None, None, :, :, None]
    mixed = mixed.reshape(B, n_chunks * CHUNK, MIX_WIDTH)[:, :T]
    return u * mixed, mq, v


def stick_breaking(q, k, v, bias, q_pos, k_pos):
    z = jnp.einsum('bqhd,bkhd->bhqk', q, k, preferred_element_type=jnp.float32) * (SB_HEAD_DIM ** -0.5)
    z = z + bias.astype(jnp.float32)[None, :, None, None]
    mask = k_pos[None, :] < q_pos[:, None]
    log_keep = jnp.where(mask, jax.nn.log_sigmoid(-z), 0.0)
    log_after = lax.cumsum(log_keep, axis=3, reverse=True) - log_keep
    a = jnp.where(mask, jnp.exp(jax.nn.log_sigmoid(z) + log_after), 0.0)
    return jnp.einsum('bhqk,bkhd->bqhd', a.astype(v.dtype), v)


def sb_project(h, w_in):
    B, T, _ = h.shape
    p = h @ w_in
    q = p[..., :MIX_WIDTH].reshape(B, T, SB_HEADS, SB_HEAD_DIM)
    k = p[..., MIX_WIDTH:2 * MIX_WIDTH].reshape(B, T, SB_HEADS, SB_HEAD_DIM)
    v = p[..., 2 * MIX_WIDTH:3 * MIX_WIDTH].reshape(B, T, SB_HEADS, SB_HEAD_DIM)
    mq = p[..., 3 * MIX_WIDTH:].reshape(B, T, MEM_HEADS, MEM_HEAD_DIM)
    return q, k, v, mq


def sb_prompt(q, k, v, bias):
    B, T, H, Dh = q.shape
    nb = T // SB_BLOCK
    qb = jnp.moveaxis(q.reshape(B, nb, SB_BLOCK, H, Dh), 1, 0)
    k_pos = jnp.arange(T)

    def block(args):
        q_blk, start = args
        return stick_breaking(q_blk, k, v, bias, start + jnp.arange(SB_BLOCK), k_pos)

    out = lax.map(block, (qb, jnp.arange(nb) * SB_BLOCK))
    return jnp.moveaxis(out, 0, 1).reshape(B, T, H * Dh)


def sb_sample(q, k, v, bias, past_k, past_v):
    Bd, Tn, H, Dh = q.shape
    past_len = past_k.shape[1] * past_k.shape[2]
    k_all = jnp.concatenate([past_k.reshape(Bd, past_len, H, Dh), k], axis=1)
    v_all = jnp.concatenate([past_v.reshape(Bd, past_len, H, Dh), v], axis=1)
    q_pos = past_len + jnp.arange(Tn)
    k_pos = jnp.arange(past_len + Tn)
    return stick_breaking(q, k_all, v_all, bias, q_pos, k_pos).reshape(Bd, Tn, H * Dh)


def mem_kv(mem, gain, w_kv, k_gain):
    B, M, _ = mem.shape
    m = rms_norm(mem, gain) @ w_kv
    k = rms_norm(m[..., :MEM_WIDTH].reshape(B, M, MEM_HEADS, MEM_HEAD_DIM), k_gain)
    v = m[..., MEM_WIDTH:].reshape(B, M, MEM_HEADS, MEM_HEAD_DIM)
    return k, v


def mem_attend(q, mk, mv, q_gain):
    B, T = q.shape[0], q.shape[1]
    q = rms_norm(q, q_gain)
    s = jnp.einsum('bqhd,bmhd->bhqm', q, mk, preferred_element_type=jnp.float32) * (MEM_HEAD_DIM ** -0.5)
    p = jax.nn.softmax(s, axis=-1)
    return jnp.einsum('bhqm,bmhd->bqhd', p.astype(mv.dtype), mv).reshape(B, T, MEM_WIDTH)


def conv_ffn(x, gain, w_up, cw, cb, w_down, past_rows):
    T = x.shape[1]
    up = rms_norm(x, gain) @ w_up
    ext = jnp.concatenate([past_rows, up], axis=1)
    conv = cb + sum(cw[j] * ext[:, j:j + T] for j in range(CONV_W))
    out = (jax.nn.silu(conv[..., :D_FF]) * conv[..., D_FF:]) @ w_down
    return out, ext[:, ext.shape[1] - (CONV_W - 1):]


def setup_inputs(seed: int = 0) -> dict:
    key = jax.random.key(seed)
    k = jax.random.split(key, 32)
    f32 = jnp.float32

    def nrm(i, shape, scale):
        return jax.random.normal(k[i], shape, f32) * scale

    n_pages = PAST_LEN // PAGE_SIZE
    n_used = DEC_BATCH * n_pages
    n_pool = n_used + max(1, n_used // 4)
    page_table = jax.random.permutation(k[0], n_pool)[:n_used].reshape(DEC_BATCH, n_pages).astype(jnp.int32)
    d_in_a = 2 * MIX_WIDTH + MEM_WIDTH
    d_in_b = 3 * MIX_WIDTH + MEM_WIDTH
    return {
        'x_prompt': nrm(1, (BATCH, SEQ, D_MODEL), 1.0),
        'x_sample': nrm(2, (DEC_BATCH, DEC_SEQ, D_MODEL), 1.0),
        'cache_mem_k': nrm(3, (DEPTH, DEC_BATCH, MEM_TOKENS, MEM_HEADS, MEM_HEAD_DIM), 1.0),
        'cache_mem_v': nrm(4, (DEPTH, DEC_BATCH, MEM_TOKENS, MEM_HEADS, MEM_HEAD_DIM), 1.0),
        'cache_sb_k': nrm(5, (N_B, n_pool, PAGE_SIZE, SB_HEADS, SB_HEAD_DIM), 1.0),
        'cache_sb_v': nrm(6, (N_B, n_pool, PAGE_SIZE, SB_HEADS, SB_HEAD_DIM), 1.0),
        'state_ffn_conv': nrm(7, (DEPTH, DEC_BATCH, CONV_W - 1, 2 * D_FF), 1.0),
        'page_table': page_table,
        'mem_prompt': nrm(8, (BATCH, MEM_TOKENS, D_MODEL), 1.0),
        'norm_mix': 1.0 + nrm(9, (DEPTH, D_MODEL), 0.05),
        'norm_ffn': 1.0 + nrm(10, (DEPTH, D_MODEL), 0.05),
        'norm_mem': 1.0 + nrm(11, (DEPTH, D_MODEL), 0.05),
        'w_in_a': nrm(12, (N_A, D_MODEL, d_in_a), D_MODEL ** -0.5),
        'gm_v_norm': 1.0 + nrm(13, (N_A, MIX_WIDTH), 0.05),
        'gm_ws': nrm(14, (N_A, GM_GROUPS, CHUNK, CHUNK), CHUNK ** -0.5),
        'gm_bs': 1.0 + nrm(15, (N_A, GM_GROUPS, CHUNK), 0.1),
        'w_in_b': nrm(16, (N_B, D_MODEL, d_in_b), D_MODEL ** -0.5),
        'sb_bias': SB_BIAS_INIT + nrm(25, (N_B, SB_HEADS), 0.1),
        'w_mem_kv': nrm(17, (DEPTH, D_MODEL, 2 * MEM_WIDTH), D_MODEL ** -0.5),
        'mem_q_norm': 1.0 + nrm(18, (DEPTH, MEM_HEAD_DIM), 0.05),
        'mem_k_norm': 1.0 + nrm(19, (DEPTH, MEM_HEAD_DIM), 0.05),
        'w_out': nrm(20, (DEPTH, D_MODEL, D_MODEL), D_MODEL ** -0.5),
        'w_up': nrm(21, (DEPTH, D_MODEL, 2 * D_FF), D_MODEL ** -0.5),
        'conv_w': nrm(22, (DEPTH, CONV_W, 2 * D_FF), CONV_W ** -0.5),
        'conv_b': nrm(23, (DEPTH, 2 * D_FF), 0.01),
        'w_down': nrm(24, (DEPTH, D_FF, D_MODEL), D_FF ** -0.5),
    }


def reference(x_prompt, x_sample, cache_mem_k, cache_mem_v, cache_sb_k, cache_sb_v, state_ffn_conv, page_table, mem_prompt, norm_mix, norm_ffn, norm_mem, w_in_a, gm_v_norm, gm_ws, gm_bs, w_in_b, sb_bias, w_mem_kv, mem_q_norm, mem_k_norm, w_out, w_up, conv_w, conv_b, w_down):
    xp, xs = x_prompt, x_sample
    mem_k_p, mem_v_p = [], []
    sb_k_p, sb_v_p, sb_k_s, sb_v_s = [], [], [], []
    gm_v_s = []
    conv_p, conv_s = [], []
    for i in range(DEPTH):
        hp = rms_norm(xp, norm_mix[i])
        hs = rms_norm(xs, norm_mix[i])
        mkp, mvp = mem_kv(mem_prompt, norm_mem[i], w_mem_kv[i], mem_k_norm[i])
        mem_k_p.append(mkp)
        mem_v_p.append(mvp)
        if i % N_MIXERS == 0:
            a = i // N_MIXERS
            mix_p, mq_p, _ = gmlp_mixer(hp, w_in_a[a], gm_v_norm[a], gm_ws[a], gm_bs[a])
            mix_s, mq_s, v_new = gmlp_mixer(hs, w_in_a[a], gm_v_norm[a], gm_ws[a], gm_bs[a])
            gm_v_s.append(v_new)
        else:
            b = i // N_MIXERS
            qp, kp, vp, mq_p = sb_project(hp, w_in_b[b])
            qs, ks, vs, mq_s = sb_project(hs, w_in_b[b])
            mix_p = sb_prompt(qp, kp, vp, sb_bias[b])
            past_k = cache_sb_k[b][page_table]
            past_v = cache_sb_v[b][page_table]
            mix_s = sb_sample(qs, ks, vs, sb_bias[b], past_k, past_v)
            sb_k_p.append(kp)
            sb_v_p.append(vp)
            sb_k_s.append(ks)
            sb_v_s.append(vs)
        mem_out_p = mem_attend(mq_p, mkp, mvp, mem_q_norm[i])
        mem_out_s = mem_attend(mq_s, cache_mem_k[i], cache_mem_v[i], mem_q_norm[i])
        xp = xp + jnp.concatenate([mix_p, mem_out_p], axis=-1) @ w_out[i]
        xs = xs + jnp.concatenate([mix_s, mem_out_s], axis=-1) @ w_out[i]
        zero_rows = jnp.zeros((xp.shape[0], CONV_W - 1, 2 * D_FF), xp.dtype)
        fp, cp = conv_ffn(xp, norm_ffn[i], w_up[i], conv_w[i], conv_b[i], w_down[i], zero_rows)
        fs, cs = conv_ffn(xs, norm_ffn[i], w_up[i], conv_w[i], conv_b[i], w_down[i], state_ffn_conv[i])
        xp = xp + fp
        xs = xs + fs
        conv_p.append(cp)
        conv_s.append(cs)
    return (xp, xs, jnp.stack(mem_k_p), jnp.stack(mem_v_p), jnp.stack(sb_k_p), jnp.stack(sb_v_p), jnp.stack(sb_k_s), jnp.stack(sb_v_s), jnp.stack(gm_v_s), jnp.stack(conv_p), jnp.stack(conv_s))
```

```python
import functools

import jax
import jax.numpy as jnp
from jax import lax
from jax.experimental import pallas as pl
from jax.experimental.pallas import tpu as pltpu

D_MODEL = 1024
MIX_WIDTH = 768
MEM_WIDTH = 256
CHUNK = 128
GM_GROUP_DIM = 128
GM_GROUPS = 6
SB_HEAD_DIM = 64
SB_HEADS = 12
MEM_TOKENS = 256
MEM_HEADS = 4
MEM_HEAD_DIM = 64
D_FF = 2816
CONV_W = 3
PAGE_SIZE = 128
EPS = 1e-6

VMEM_LIMIT = 56 * 1024 * 1024
F32 = jnp.float32
BF16 = jnp.bfloat16


def _cparams(sem):
    return pltpu.CompilerParams(dimension_semantics=sem, vmem_limit_bytes=VMEM_LIMIT)


def _rms(x, gain):
    ms = jnp.mean(x * x, axis=-1, keepdims=True)
    return x * lax.rsqrt(ms + EPS) * gain


def _nt_dot(a, b):
    return lax.dot_general(a, b, (((1,), (1,)), ((), ())), preferred_element_type=F32)


def _dot(a, b):
    return jnp.dot(a, b, preferred_element_type=F32)


def _split(x):
    hi = x.astype(BF16)
    return hi, (x - hi.astype(F32)).astype(BF16)


def _same_head(width, head_dim):
    r = lax.broadcasted_iota(jnp.int32, (width, width), 0) // head_dim
    c = lax.broadcasted_iota(jnp.int32, (width, width), 1) // head_dim
    return (r == c).astype(BF16)


def _head_stack(q, n_heads, head_dim):
    q = q.astype(F32)
    head = lax.broadcasted_iota(jnp.int32, q.shape, 1) // head_dim
    return jnp.concatenate([jnp.where(head == h, q, 0.0) for h in range(n_heads)], axis=0).astype(BF16)


def _head_unstack(o, n_heads, head_dim):
    t = o.shape[0] // n_heads
    head = lax.broadcasted_iota(jnp.int32, (t, o.shape[1]), 1) // head_dim
    out = jnp.zeros((t, o.shape[1]), o.dtype)
    for h in range(n_heads):
        out = out + jnp.where(head == h, o[h * t:(h + 1) * t], 0.0)
    return out


def _mem_kv_kernel(mem_ref, gain_ref, wt_ref, kgain_ref, k_ref, v_ref):
    h = _rms(mem_ref[0], gain_ref[0]).astype(BF16)
    mt = _nt_dot(wt_ref[0], h)
    kt = mt[:MEM_WIDTH]
    hi, lo = _split(kt * kt)
    same = _same_head(MEM_WIDTH, MEM_HEAD_DIM)
    ss = _dot(same, hi) + _dot(same, lo)
    k_ref[0, 0] = kt * lax.rsqrt(ss * (1.0 / MEM_HEAD_DIM) + EPS) * kgain_ref[0]
    v_ref[0, 0] = mt[MEM_WIDTH:]


def _mem_kv(mem, norm_mem, w_kv_t_bf, k_gain_col):
    depth = norm_mem.shape[0]
    batch, m_tok, d = mem.shape
    out = jax.ShapeDtypeStruct((depth, batch, MEM_WIDTH, m_tok), F32)
    return pl.pallas_call(
        _mem_kv_kernel,
        grid=(depth, batch),
        in_specs=[
            pl.BlockSpec((1, m_tok, d), lambda i, b: (b, 0, 0)),
            pl.BlockSpec((1, 1, d), lambda i, b: (i, 0, 0)),
            pl.BlockSpec((1, 2 * MEM_WIDTH, d), lambda i, b: (i, 0, 0)),
            pl.BlockSpec((1, MEM_WIDTH, 1), lambda i, b: (i, 0, 0)),
        ],
        out_specs=[
            pl.BlockSpec((1, 1, MEM_WIDTH, m_tok), lambda i, b: (i, b, 0, 0)),
            pl.BlockSpec((1, 1, MEM_WIDTH, m_tok), lambda i, b: (i, b, 0, 0)),
        ],
        out_shape=[out, out],
        compiler_params=_cparams(("arbitrary", "arbitrary")),
        name="mem_kv",
    )(mem, norm_mem.reshape(depth, 1, d), w_kv_t_bf, k_gain_col)


def _gmlp_in_kernel(x_ref, gain_ref, w_ref, vgain_ref, wmix_ref, bias_ref, mix_ref, mq_ref, v_ref, *, seq_len, cm):
    tm = x_ref.shape[0]
    h = _rms(x_ref[...], gain_ref[...]).astype(BF16)
    proj = _dot(h, w_ref[...])
    mq_ref[...] = proj[:, 2 * MIX_WIDTH:]
    v_ref[...] = _rms(jax.nn.gelu(proj[:, MIX_WIDTH:2 * MIX_WIDTH]), vgain_ref[...])
    mix_ref[...] = jax.nn.gelu(proj[:, :MIX_WIDTH])
    r = lax.broadcasted_iota(jnp.int32, (cm, cm), 0)
    c = lax.broadcasted_iota(jnp.int32, (cm, cm), 1)
    allowed = (r // seq_len == c // seq_len) & (c % seq_len <= r % seq_len)
    for g in range(GM_GROUPS):
        cols = slice(g * GM_GROUP_DIM, (g + 1) * GM_GROUP_DIM)
        wg = jnp.where(allowed, wmix_ref[g], 0.0).astype(BF16)
        for ci in range(tm // cm):
            rows = slice(ci * cm, (ci + 1) * cm)
            mixed = _dot(wg, v_ref[rows, cols].astype(BF16)) + bias_ref[:, cols]
            mix_ref[rows, cols] = mix_ref[rows, cols] * mixed


def _gmlp_in(x, gain, w_bf, v_gain, wmix, bias_full, *, seq_len, cm, tm, emit_v):
    n, d = x.shape
    d_in = w_bf.shape[1]
    const = lambda i: (0, 0)
    out_shape = [jax.ShapeDtypeStruct((n, MIX_WIDTH), F32), jax.ShapeDtypeStruct((n, MEM_WIDTH), F32)]
    out_specs = [pl.BlockSpec((tm, MIX_WIDTH), lambda i: (i, 0)), pl.BlockSpec((tm, MEM_WIDTH), lambda i: (i, 0))]
    scratch = []
    if emit_v:
        out_shape.append(jax.ShapeDtypeStruct((n, MIX_WIDTH), F32))
        out_specs.append(pl.BlockSpec((tm, MIX_WIDTH), lambda i: (i, 0)))
    else:
        scratch.append(pltpu.VMEM((tm, MIX_WIDTH), F32))
    return pl.pallas_call(
        functools.partial(_gmlp_in_kernel, seq_len=seq_len, cm=cm),
        grid=(n // tm,),
        in_specs=[
            pl.BlockSpec((tm, d), lambda i: (i, 0)),
            pl.BlockSpec((1, d), const),
            pl.BlockSpec((d, d_in), const),
            pl.BlockSpec((1, MIX_WIDTH), const),
            pl.BlockSpec((GM_GROUPS, cm, cm), lambda i: (0, 0, 0)),
            pl.BlockSpec((cm, MIX_WIDTH), const),
        ],
        out_specs=out_specs,
        out_shape=out_shape,
        scratch_shapes=scratch,
        compiler_params=_cparams(("arbitrary",)),
        name="gmlp_in",
    )(x, gain.reshape(1, d), w_bf, v_gain.reshape(1, MIX_WIDTH), wmix, bias_full)


def _sb_in_rows_kernel(x_ref, gain_ref, w_ref, q_ref, k_ref, v_ref, mq_ref):
    h = _rms(x_ref[...], gain_ref[...]).astype(BF16)
    proj = _dot(h, w_ref[...])
    q_ref[...] = (proj[:, :MIX_WIDTH] * (SB_HEAD_DIM ** -0.5)).astype(BF16)
    k_ref[...] = proj[:, MIX_WIDTH:2 * MIX_WIDTH]
    v_ref[...] = proj[:, 2 * MIX_WIDTH:3 * MIX_WIDTH]
    mq_ref[...] = proj[:, 3 * MIX_WIDTH:]


def _sb_in_rows(x, gain, w_bf):
    n, d = x.shape
    full = lambda width: pl.BlockSpec((n, width), lambda i: (0, 0))
    shp = lambda width, dt: jax.ShapeDtypeStruct((n, width), dt)
    return pl.pallas_call(
        _sb_in_rows_kernel,
        grid=(1,),
        in_specs=[full(d), pl.BlockSpec((1, d), lambda i: (0, 0)), pl.BlockSpec(w_bf.shape, lambda i: (0, 0))],
        out_specs=[full(MIX_WIDTH)] * 3 + [full(MEM_WIDTH)],
        out_shape=[shp(MIX_WIDTH, BF16), shp(MIX_WIDTH, F32), shp(MIX_WIDTH, F32), shp(MEM_WIDTH, F32)],
        compiler_params=_cparams(("arbitrary",)),
        name="sb_in_rows",
    )(x, gain.reshape(1, d), w_bf)


def _sb_in_prompt_kernel(x_ref, gain_ref, wq_ref, wkvt_ref, q_ref, kt_ref, vt_ref, ktb_ref, vtb_ref, mq_ref, *, tk):
    tm = x_ref.shape[1]
    h = _rms(x_ref[0], gain_ref[...]).astype(BF16)
    qm = _dot(h, wq_ref[...])
    q_ref[0] = (qm[:, :MIX_WIDTH] * (SB_HEAD_DIM ** -0.5)).astype(BF16)
    mq_ref[0] = qm[:, MIX_WIDTH:]
    kvt = _nt_dot(wkvt_ref[...], h)
    kt_ref[0] = kvt[:MIX_WIDTH]
    vt_ref[0] = kvt[MIX_WIDTH:]
    for c in range(tm // tk):
        ktb_ref[0, c] = kvt[:MIX_WIDTH, c * tk:(c + 1) * tk].astype(BF16)
        vtb_ref[0, c] = kvt[MIX_WIDTH:, c * tk:(c + 1) * tk].astype(BF16)


def _sb_in_prompt(x, gain, wq_bf, wkvt_bf, *, tm, tk):
    b, t, d = x.shape
    const = lambda bi, i: (0, 0)
    return pl.pallas_call(
        functools.partial(_sb_in_prompt_kernel, tk=tk),
        grid=(b, t // tm),
        in_specs=[
            pl.BlockSpec((1, tm, d), lambda bi, i: (bi, i, 0)),
            pl.BlockSpec((1, d), const),
            pl.BlockSpec((d, MIX_WIDTH + MEM_WIDTH), const),
            pl.BlockSpec((2 * MIX_WIDTH, d), const),
        ],
        out_specs=[
            pl.BlockSpec((1, tm, MIX_WIDTH), lambda bi, i: (bi, i, 0)),
            pl.BlockSpec((1, MIX_WIDTH, tm), lambda bi, i: (bi, 0, i)),
            pl.BlockSpec((1, MIX_WIDTH, tm), lambda bi, i: (bi, 0, i)),
            pl.BlockSpec((1, tm // tk, MIX_WIDTH, tk), lambda bi, i: (bi, i, 0, 0)),
            pl.BlockSpec((1, tm // tk, MIX_WIDTH, tk), lambda bi, i: (bi, i, 0, 0)),
            pl.BlockSpec((1, tm, MEM_WIDTH), lambda bi, i: (bi, i, 0)),
        ],
        out_shape=[
            jax.ShapeDtypeStruct((b, t, MIX_WIDTH), BF16),
            jax.ShapeDtypeStruct((b, MIX_WIDTH, t), F32),
            jax.ShapeDtypeStruct((b, MIX_WIDTH, t), F32),
            jax.ShapeDtypeStruct((b, t // tk, MIX_WIDTH, tk), BF16),
            jax.ShapeDtypeStruct((b, t // tk, MIX_WIDTH, tk), BF16),
            jax.ShapeDtypeStruct((b, t, MEM_WIDTH), F32),
        ],
        compiler_params=_cparams(("arbitrary", "arbitrary")),
        name="sb_in_prompt",
    )(x, gain.reshape(1, d), wq_bf, wkvt_bf)


def _sb_block(q_stack, k, v, bias_col, tri, carry, mask, feature_major):
    z = (_dot(q_stack, k) if feature_major else _nt_dot(q_stack, k)) + bias_col
    sp = jnp.maximum(z, 0.0) + jnp.log1p(jnp.exp(-jnp.abs(z)))
    log_keep = -sp
    if mask is not None:
        log_keep = jnp.where(mask, log_keep, 0.0)
    hi, lo = _split(log_keep)
    log_after = _dot(hi, tri) + _dot(lo, tri) + carry
    a = jnp.exp(z - sp + log_after)
    if mask is not None:
        a = jnp.where(mask, a, 0.0)
    a = a.astype(BF16)
    contrib = _nt_dot(a, v) if feature_major else _dot(a, v)
    return contrib, carry + jnp.sum(log_keep, axis=1, keepdims=True)


def _tri(tk):
    j = lax.broadcasted_iota(jnp.int32, (tk, tk), 0)
    s = lax.broadcasted_iota(jnp.int32, (tk, tk), 1)
    return (j > s).astype(BF16)


HEADS_PER_GROUP = 4
GROUP_WIDTH = HEADS_PER_GROUP * SB_HEAD_DIM


def _sb_prompt_kernel(bias_ref, q_ref, k_ref, v_ref, o_ref, acc_ref, carry_ref, *, tq):
    hg = pl.program_id(1)
    qi = pl.program_id(2)
    rows = HEADS_PER_GROUP * tq
    q_stack = _head_stack(q_ref[0], HEADS_PER_GROUP, SB_HEAD_DIM)
    bias_col = jnp.concatenate(
        [jnp.full((tq, 1), bias_ref[hg * HEADS_PER_GROUP + h], F32) for h in range(HEADS_PER_GROUP)], axis=0)
    tri = _tri(tq)
    t_row = lax.broadcasted_iota(jnp.int32, (rows, tq), 0) % tq
    s_col = lax.broadcasted_iota(jnp.int32, (rows, tq), 1)

    contrib, carry = _sb_block(q_stack, k_ref[0, qi], v_ref[0, qi], bias_col, tri,
                               jnp.zeros((rows, 1), F32), s_col < t_row, True)
    acc_ref[...] = contrib
    carry_ref[...] = carry

    def body(i, _):
        kj = qi - 1 - i
        contrib, carry = _sb_block(q_stack, k_ref[0, kj], v_ref[0, kj], bias_col, tri, carry_ref[...], None, True)
        acc_ref[...] += contrib
        carry_ref[...] = carry
        return 0

    lax.fori_loop(0, qi, body, 0)
    o_ref[0] = _head_unstack(acc_ref[...], HEADS_PER_GROUP, SB_HEAD_DIM)


def _sb_prompt(q_bf, kt_bf, vt_bf, bias, *, tq):
    b, t, w = q_bf.shape
    nkb = t // tq
    return pl.pallas_call(
        functools.partial(_sb_prompt_kernel, tq=tq),
        grid=(b, w // GROUP_WIDTH, nkb),
        in_specs=[
            pl.BlockSpec(memory_space=pltpu.SMEM),
            pl.BlockSpec((1, tq, GROUP_WIDTH), lambda bi, g, qi: (bi, qi, g)),
            pl.BlockSpec((1, nkb, GROUP_WIDTH, tq), lambda bi, g, qi: (bi, 0, g, 0)),
            pl.BlockSpec((1, nkb, GROUP_WIDTH, tq), lambda bi, g, qi: (bi, 0, g, 0)),
        ],
        out_specs=pl.BlockSpec((1, tq, GROUP_WIDTH), lambda bi, g, qi: (bi, qi, g)),
        out_shape=jax.ShapeDtypeStruct((b, t, w), F32),
        scratch_shapes=[pltpu.VMEM((HEADS_PER_GROUP * tq, GROUP_WIDTH), F32),
                        pltpu.VMEM((HEADS_PER_GROUP * tq, 1), F32)],
        compiler_params=_cparams(("arbitrary", "arbitrary", "arbitrary")),
        name="sb_prompt",
    )(bias, q_bf, kt_bf, vt_bf)


PAGES_PER_STEP = 8


def _sb_decode_kernel(pt_ref, bias_ref, q_ref, kn_ref, vn_ref, *refs, t_new):
    k_pages = refs[:PAGES_PER_STEP]
    v_pages = refs[PAGES_PER_STEP:2 * PAGES_PER_STEP]
    o_ref, acc_ref, carry_ref = refs[2 * PAGES_PER_STEP:]
    j = pl.program_id(1)
    rows = SB_HEADS * t_new
    q_stack = _head_stack(q_ref[0], SB_HEADS, SB_HEAD_DIM)
    bias_col = jnp.concatenate([jnp.full((t_new, 1), bias_ref[h], F32) for h in range(SB_HEADS)], axis=0)

    @pl.when(j == 0)
    def _():
        pad = jnp.zeros((PAGE_SIZE - t_new, MIX_WIDTH), F32)
        k = jnp.concatenate([kn_ref[0], pad], axis=0).astype(BF16)
        v = jnp.concatenate([vn_ref[0], pad], axis=0).astype(BF16)
        t_row = lax.broadcasted_iota(jnp.int32, (rows, PAGE_SIZE), 0) % t_new
        s_col = lax.broadcasted_iota(jnp.int32, (rows, PAGE_SIZE), 1)
        contrib, carry = _sb_block(q_stack, k, v, bias_col, _tri(PAGE_SIZE),
                                   jnp.zeros((rows, 1), F32), s_col < t_row, False)
        acc_ref[...] = contrib
        carry_ref[...] = carry

    tri = _tri(2 * PAGE_SIZE)
    for p in range(PAGES_PER_STEP - 2, -2, -2):
        k = jnp.concatenate([k_pages[p][0], k_pages[p + 1][0]], axis=1).astype(BF16)
        v = jnp.concatenate([v_pages[p][0], v_pages[p + 1][0]], axis=1).astype(BF16)
        contrib, carry = _sb_block(q_stack, k, v, bias_col, tri, carry_ref[...], None, True)
        acc_ref[...] += contrib
        carry_ref[...] = carry

    @pl.when(j == pl.num_programs(1) - 1)
    def _():
        o_ref[0] = _head_unstack(acc_ref[...], SB_HEADS, SB_HEAD_DIM)


def _sb_decode(q_bf, k_new, v_new, cache_kt, cache_vt, page_table, bias):
    b, t_new, w = q_bf.shape
    n_pages = page_table.shape[1]
    n_steps = n_pages // PAGES_PER_STEP

    def page_spec(slot):
        def index_map(bi, j, pt, bias):
            return (pt[bi, (n_steps - 1 - j) * PAGES_PER_STEP + slot], 0, 0)
        return pl.BlockSpec((1, w, PAGE_SIZE), index_map)

    new_spec = pl.BlockSpec((1, t_new, w), lambda bi, j, pt, bias: (bi, 0, 0))
    grid_spec = pltpu.PrefetchScalarGridSpec(
        num_scalar_prefetch=2,
        grid=(b, n_steps),
        in_specs=[new_spec, new_spec, new_spec]
        + [page_spec(s) for s in range(PAGES_PER_STEP)] * 2,
        out_specs=new_spec,
        scratch_shapes=[pltpu.VMEM((SB_HEADS * t_new, w), F32), pltpu.VMEM((SB_HEADS * t_new, 1), F32)],
    )
    return pl.pallas_call(
        functools.partial(_sb_decode_kernel, t_new=t_new),
        grid_spec=grid_spec,
        out_shape=jax.ShapeDtypeStruct((b, t_new, w), F32),
        compiler_params=_cparams(("arbitrary", "arbitrary")),
        name="sb_decode",
    )(page_table, bias, q_bf, k_new, v_new, *([cache_kt] * PAGES_PER_STEP), *([cache_vt] * PAGES_PER_STEP))


def _mem_attend_kernel(q_ref, kt_ref, vt_ref, gain_ref, o_ref):
    q = q_ref[0]
    hi, lo = _split(q * q)
    same = _same_head(MEM_WIDTH, MEM_HEAD_DIM)
    ss = _dot(hi, same) + _dot(lo, same)
    qn = q * lax.rsqrt(ss * (1.0 / MEM_HEAD_DIM) + EPS) * gain_ref[...]
    q_stack = _head_stack(qn * (MEM_HEAD_DIM ** -0.5), MEM_HEADS, MEM_HEAD_DIM)
    s = _dot(q_stack, kt_ref[0].astype(BF16))
    p = jnp.exp(s - jnp.max(s, axis=-1, keepdims=True))
    l = jnp.sum(p, axis=-1, keepdims=True)
    o = _nt_dot(p.astype(BF16), vt_ref[0].astype(BF16)) / l
    o_ref[0] = _head_unstack(o, MEM_HEADS, MEM_HEAD_DIM)


def _mem_attend(q, mkt, mvt, q_gain_full, *, tm):
    b, t, w = q.shape
    return pl.pallas_call(
        _mem_attend_kernel,
        grid=(b, t // tm),
        in_specs=[
            pl.BlockSpec((1, tm, w), lambda bi, i: (bi, i, 0)),
            pl.BlockSpec((1, w, MEM_TOKENS), lambda bi, i: (bi, 0, 0)),
            pl.BlockSpec((1, w, MEM_TOKENS), lambda bi, i: (bi, 0, 0)),
            pl.BlockSpec((1, w), lambda bi, i: (0, 0)),
        ],
        out_specs=pl.BlockSpec((1, tm, w), lambda bi, i: (bi, i, 0)),
        out_shape=jax.ShapeDtypeStruct((b, t, w), F32),
        compiler_params=_cparams(("arbitrary", "arbitrary")),
        name="mem_attend",
    )(q, mkt, mvt, q_gain_full.reshape(1, w))


def _out_proj_kernel(mix_ref, mem_ref, x_ref, w_ref, y_ref):
    y = _dot(mix_ref[...].astype(BF16), w_ref[:MIX_WIDTH, :])
    y = y + _dot(mem_ref[...].astype(BF16), w_ref[MIX_WIDTH:, :])
    y_ref[...] = x_ref[...] + y


def _out_proj(mix, mem_out, x, w_bf, *, tm):
    n, d = x.shape
    return pl.pallas_call(
        _out_proj_kernel,
        grid=(n // tm,),
        in_specs=[
            pl.BlockSpec((tm, MIX_WIDTH), lambda i: (i, 0)),
            pl.BlockSpec((tm, MEM_WIDTH), lambda i: (i, 0)),
            pl.BlockSpec((tm, d), lambda i: (i, 0)),
            pl.BlockSpec((d, d), lambda i: (0, 0)),
        ],
        out_specs=pl.BlockSpec((tm, d), lambda i: (i, 0)),
        out_shape=jax.ShapeDtypeStruct((n, d), F32),
        compiler_params=_cparams(("arbitrary",)),
        name="out_proj",
    )(mix, mem_out, x, w_bf)


FF_CHUNK = 256
EXT_PAD = 8


def _conv_ffn_kernel(x_ref, gain_ref, wup_ref, cw_ref, cb_ref, wdown_ref, past_ref,
                     y_ref, state_ref, ext_ref, h_ref, acc_ref):
    t = pl.program_id(1)
    tm = x_ref.shape[1]
    lo = EXT_PAD - (CONV_W - 1)

    @pl.when(t == 0)
    def _():
        ext_ref[lo:EXT_PAD, :] = past_ref[0]

    @pl.when(t > 0)
    def _():
        ext_ref[lo:EXT_PAD, :] = ext_ref[tm + lo:tm + EXT_PAD, :]

    x = x_ref[0]
    h_ref[...] = _rms(x, gain_ref[...]).astype(BF16)
    acc_ref[...] = x
    for c in range(D_FF // FF_CHUNK):
        conv = []
        for half in range(2):
            cols = slice(half * D_FF + c * FF_CHUNK, half * D_FF + (c + 1) * FF_CHUNK)
            up = _dot(h_ref[...], wup_ref[:, cols])
            ext_ref[EXT_PAD:EXT_PAD + tm, cols] = up
            y = cb_ref[:, cols] + cw_ref[CONV_W - 1:CONV_W, cols] * up
            for j in range(CONV_W - 1):
                y = y + cw_ref[j:j + 1, cols] * ext_ref[lo + j:lo + j + tm, cols]
            conv.append(y)
        g = (jax.nn.silu(conv[0]) * conv[1]).astype(BF16)
        acc_ref[...] += _dot(g, wdown_ref[c * FF_CHUNK:(c + 1) * FF_CHUNK, :])
    y_ref[0] = acc_ref[...]
    state_ref[0] = ext_ref[tm + lo:tm + EXT_PAD, :]


def _conv_ffn(x, gain, wup_bf, cw, cb, wdown_bf, past, *, tm):
    b, t, d = x.shape
    ff2 = 2 * D_FF
    const = lambda bi, i: (0, 0)
    return pl.pallas_call(
        _conv_ffn_kernel,
        grid=(b, t // tm),
        in_specs=[
            pl.BlockSpec((1, tm, d), lambda bi, i: (bi, i, 0)),
            pl.BlockSpec((1, d), const),
            pl.BlockSpec((d, ff2), const, pipeline_mode=pl.Buffered(1)),
            pl.BlockSpec((CONV_W, ff2), const),
            pl.BlockSpec((1, ff2), const),
            pl.BlockSpec((D_FF, d), const, pipeline_mode=pl.Buffered(1)),
            pl.BlockSpec((1, CONV_W - 1, ff2), lambda bi, i: (bi, 0, 0)),
        ],
        out_specs=[
            pl.BlockSpec((1, tm, d), lambda bi, i: (bi, i, 0)),
            pl.BlockSpec((1, CONV_W - 1, ff2), lambda bi, i: (bi, 0, 0)),
        ],
        out_shape=[jax.ShapeDtypeStruct((b, t, d), F32), jax.ShapeDtypeStruct((b, CONV_W - 1, ff2), F32)],
        scratch_shapes=[pltpu.VMEM((EXT_PAD + tm, ff2), F32), pltpu.VMEM((tm, d), BF16), pltpu.VMEM((tm, d), F32)],
        compiler_params=_cparams(("arbitrary", "arbitrary")),
        name="conv_ffn",
    )(x, gain.reshape(1, d), wup_bf, cw, cb.reshape(1, ff2), wdown_bf, past)


def _feature_major(x):
    lead = x.shape[:-3]
    n, h, dh = x.shape[-3:]
    nd = len(lead)
    return jnp.transpose(x, (*range(nd), nd + 1, nd + 2, nd)).reshape(*lead, h * dh, n)


def _position_major(xt, n_heads):
    lead = xt.shape[:-2]
    w, n = xt.shape[-2:]
    nd = len(lead)
    x4 = xt.reshape(*lead, n_heads, w // n_heads, n)
    return jnp.transpose(x4, (*range(nd), nd + 2, nd, nd + 1))


def kernel(x_prompt, x_sample, cache_mem_k, cache_mem_v, cache_sb_k, cache_sb_v, state_ffn_conv, page_table, mem_prompt, norm_mix, norm_ffn, norm_mem, w_in_a, gm_v_norm, gm_ws, gm_bs, w_in_b, sb_bias, w_mem_kv, mem_q_norm, mem_k_norm, w_out, w_up, conv_w, conv_b, w_down):
    depth = norm_mix.shape[0]
    bp, tp, d = x_prompt.shape
    bs_, ts, _ = x_sample.shape
    n_p, n_s = bp * tp, bs_ * ts
    sb_tile = 256

    k_gain_col = jnp.tile(mem_k_norm, (1, MEM_HEADS)).reshape(depth, MEM_WIDTH, 1)
    mem_kt_p, mem_vt_p = _mem_kv(mem_prompt, norm_mem, jnp.swapaxes(w_mem_kv, 1, 2).astype(BF16), k_gain_col)
    q_gain_full = jnp.tile(mem_q_norm, (1, MEM_HEADS))
    mem_kt_s = _feature_major(cache_mem_k)
    mem_vt_s = _feature_major(cache_mem_v)

    xp = x_prompt.reshape(n_p, d)
    xs = x_sample.reshape(n_s, d)
    sb_k_p, sb_v_p, sb_k_s, sb_v_s, gm_v_s, conv_p, conv_s = [], [], [], [], [], [], []
    for i in range(depth):
        if i % 2 == 0:
            a = i // 2
            w_bf = w_in_a[a].astype(BF16)
            bias_full = jnp.repeat(gm_bs[a].T, GM_GROUP_DIM, axis=1)
            mix_p, mq_p = _gmlp_in(xp, norm_mix[i], w_bf, gm_v_norm[a], gm_ws[a], bias_full,
                                   seq_len=CHUNK, cm=CHUNK, tm=512, emit_v=False)
            reps = n_s // ts
            wmix_s = jnp.tile(gm_ws[a][:, :ts, :ts], (1, reps, reps))
            bias_s = jnp.tile(bias_full[:ts], (reps, 1))
            mix_s, mq_s, v_new = _gmlp_in(xs, norm_mix[i], w_bf, gm_v_norm[a], wmix_s, bias_s,
                                          seq_len=ts, cm=n_s, tm=n_s, emit_v=True)
            gm_v_s.append(v_new.reshape(bs_, ts, MIX_WIDTH))
        else:
            b = i // 2
            w = w_in_b[b]
            wq_bf = jnp.concatenate([w[:, :MIX_WIDTH], w[:, 3 * MIX_WIDTH:]], axis=1).astype(BF16)
            wkvt_bf = w[:, MIX_WIDTH:3 * MIX_WIDTH].T.astype(BF16)
            q_p, kt_p, vt_p, ktb_p, vtb_p, mq_p = _sb_in_prompt(
                xp.reshape(bp, tp, d), norm_mix[i], wq_bf, wkvt_bf, tm=512, tk=sb_tile)
            mq_p = mq_p.reshape(n_p, MEM_WIDTH)
            q_s, k_s, v_s, mq_s = _sb_in_rows(xs, norm_mix[i], w.astype(BF16))
            mix_p = _sb_prompt(q_p, ktb_p, vtb_p, sb_bias[b], tq=sb_tile).reshape(n_p, MIX_WIDTH)
            mix_s = _sb_decode(q_s.reshape(bs_, ts, MIX_WIDTH), k_s.reshape(bs_, ts, MIX_WIDTH),
                               v_s.reshape(bs_, ts, MIX_WIDTH),
                               _feature_major(cache_sb_k[b]), _feature_major(cache_sb_v[b]),
                               page_table, sb_bias[b]).reshape(n_s, MIX_WIDTH)
            sb_k_p.append(_position_major(kt_p, SB_HEADS))
            sb_v_p.append(_position_major(vt_p, SB_HEADS))
            sb_k_s.append(k_s.reshape(bs_, ts, SB_HEADS, SB_HEAD_DIM))
            sb_v_s.append(v_s.reshape(bs_, ts, SB_HEADS, SB_HEAD_DIM))
        mem_out_p = _mem_attend(mq_p.reshape(bp, tp, MEM_WIDTH), mem_kt_p[i], mem_vt_p[i], q_gain_full[i], tm=256)
        mem_out_s = _mem_attend(mq_s.reshape(bs_, ts, MEM_WIDTH), mem_kt_s[i], mem_vt_s[i], q_gain_full[i], tm=ts)
        w_out_bf = w_out[i].astype(BF16)
        xp = _out_proj(mix_p, mem_out_p.reshape(n_p, MEM_WIDTH), xp, w_out_bf, tm=512)
        xs = _out_proj(mix_s, mem_out_s.reshape(n_s, MEM_WIDTH), xs, w_out_bf, tm=n_s)
        w_up_bf = w_up[i].astype(BF16)
        w_down_bf = w_down[i].astype(BF16)
        zero_rows = jnp.zeros((bp, CONV_W - 1, 2 * D_FF), F32)
        xp3, cp = _conv_ffn(xp.reshape(bp, tp, d), norm_ffn[i], w_up_bf, conv_w[i], conv_b[i], w_down_bf,
                            zero_rows, tm=256)
        xs3, cs = _conv_ffn(xs.reshape(bs_, ts, d), norm_ffn[i], w_up_bf, conv_w[i], conv_b[i], w_down_bf,
                            state_ffn_conv[i], tm=ts)
        xp = xp3.reshape(n_p, d)
        xs = xs3.reshape(n_s, d)
        conv_p.append(cp)
        conv_s.append(cs)

    return (xp.reshape(bp, tp, d), xs.reshape(bs_, ts, d),
            _position_major(mem_kt_p, MEM_HEADS), _position_major(mem_vt_p, MEM_HEADS),
            jnp.stack(sb_k_p), jnp.stack(sb_v_p), jnp.stack(sb_k_s), jnp.stack(sb_v_s),
            jnp.stack(gm_v_s), jnp.stack(conv_p), jnp.stack(conv_s))
```

```python
import functools

import jax
import jax.numpy as jnp
from jax import lax
from jax.experimental import pallas as pl
from jax.experimental.pallas import tpu as pltpu

D_MODEL = 1024
MIX_WIDTH = 768
MEM_WIDTH = 256
CHUNK = 128
GM_GROUP_DIM = 128
GM_GROUPS = 6
SB_HEAD_DIM = 64
SB_HEADS = 12
MEM_TOKENS = 256
MEM_HEADS = 4
MEM_HEAD_DIM = 64
D_FF = 2816
CONV_W = 3
PAGE_SIZE = 128
EPS = 1e-6

VMEM_LIMIT = 56 * 1024 * 1024
F32 = jnp.float32
BF16 = jnp.bfloat16
LOG2E = 1.4426950408889634
SB_Q_SCALE = SB_HEAD_DIM ** -0.5 * LOG2E


def _cparams(sem):
    return pltpu.CompilerParams(dimension_semantics=sem, vmem_limit_bytes=VMEM_LIMIT)


def _rms(x, gain):
    ms = jnp.mean(x * x, axis=-1, keepdims=True)
    return x * lax.rsqrt(ms + EPS) * gain


def _nt_dot(a, b):
    return lax.dot_general(a, b, (((1,), (1,)), ((), ())), preferred_element_type=F32)


def _dot(a, b):
    return jnp.dot(a, b, preferred_element_type=F32)


def _split(x):
    hi = x.astype(BF16)
    return hi, (x - hi.astype(F32)).astype(BF16)


def _same_head(width, head_dim):
    r = lax.broadcasted_iota(jnp.int32, (width, width), 0) // head_dim
    c = lax.broadcasted_iota(jnp.int32, (width, width), 1) // head_dim
    return (r == c).astype(BF16)


def _head_stack(q, n_heads, head_dim):
    q = q.astype(F32)
    head = lax.broadcasted_iota(jnp.int32, q.shape, 1) // head_dim
    return jnp.concatenate([jnp.where(head == h, q, 0.0) for h in range(n_heads)], axis=0).astype(BF16)


def _head_unstack(o, n_heads, head_dim):
    t = o.shape[0] // n_heads
    head = lax.broadcasted_iota(jnp.int32, (t, o.shape[1]), 1) // head_dim
    out = jnp.zeros((t, o.shape[1]), o.dtype)
    for h in range(n_heads):
        out = out + jnp.where(head == h, o[h * t:(h + 1) * t], 0.0)
    return out


def _mem_kv_kernel(mem_ref, gain_ref, wt_ref, kgain_ref, k_ref, v_ref):
    h = _rms(mem_ref[0], gain_ref[0]).astype(BF16)
    mt = _nt_dot(wt_ref[0], h)
    kt = mt[:MEM_WIDTH]
    hi, lo = _split(kt * kt)
    same = _same_head(MEM_WIDTH, MEM_HEAD_DIM)
    ss = _dot(same, hi) + _dot(same, lo)
    k_ref[0, 0] = kt * lax.rsqrt(ss * (1.0 / MEM_HEAD_DIM) + EPS) * kgain_ref[0]
    v_ref[0, 0] = mt[MEM_WIDTH:]


def _mem_kv(mem, norm_mem, w_kv_t_bf, k_gain_col):
    depth = norm_mem.shape[0]
    batch, m_tok, d = mem.shape
    out = jax.ShapeDtypeStruct((depth, batch, MEM_WIDTH, m_tok), F32)
    return pl.pallas_call(
        _mem_kv_kernel,
        grid=(depth, batch),
        in_specs=[
            pl.BlockSpec((1, m_tok, d), lambda i, b: (b, 0, 0)),
            pl.BlockSpec((1, 1, d), lambda i, b: (i, 0, 0)),
            pl.BlockSpec((1, 2 * MEM_WIDTH, d), lambda i, b: (i, 0, 0)),
            pl.BlockSpec((1, MEM_WIDTH, 1), lambda i, b: (i, 0, 0)),
        ],
        out_specs=[
            pl.BlockSpec((1, 1, MEM_WIDTH, m_tok), lambda i, b: (i, b, 0, 0)),
            pl.BlockSpec((1, 1, MEM_WIDTH, m_tok), lambda i, b: (i, b, 0, 0)),
        ],
        out_shape=[out, out],
        compiler_params=_cparams(("arbitrary", "arbitrary")),
        name="mem_kv",
    )(mem, norm_mem.reshape(depth, 1, d), w_kv_t_bf, k_gain_col)


def _gmlp_in_kernel(x_ref, gain_ref, w_ref, vgain_ref, wmix_ref, bias_ref, mix_ref, mq_ref, v_ref, *, seq_len, cm):
    tm = x_ref.shape[0]
    h = _rms(x_ref[...], gain_ref[...]).astype(BF16)
    proj = _dot(h, w_ref[...])
    mq_ref[...] = proj[:, 2 * MIX_WIDTH:]
    v_ref[...] = _rms(jax.nn.gelu(proj[:, MIX_WIDTH:2 * MIX_WIDTH]), vgain_ref[...])
    mix_ref[...] = jax.nn.gelu(proj[:, :MIX_WIDTH])
    r = lax.broadcasted_iota(jnp.int32, (cm, cm), 0)
    c = lax.broadcasted_iota(jnp.int32, (cm, cm), 1)
    allowed = (r // seq_len == c // seq_len) & (c % seq_len <= r % seq_len)
    for g in range(GM_GROUPS):
        cols = slice(g * GM_GROUP_DIM, (g + 1) * GM_GROUP_DIM)
        wg = jnp.where(allowed, wmix_ref[g], 0.0).astype(BF16)
        for ci in range(tm // cm):
            rows = slice(ci * cm, (ci + 1) * cm)
            mixed = _dot(wg, v_ref[rows, cols].astype(BF16)) + bias_ref[:, cols]
            mix_ref[rows, cols] = mix_ref[rows, cols] * mixed


def _gmlp_in(x, gain, w_bf, v_gain, wmix, bias_full, *, seq_len, cm, tm, emit_v):
    n, d = x.shape
    d_in = w_bf.shape[1]
    const = lambda i: (0, 0)
    out_shape = [jax.ShapeDtypeStruct((n, MIX_WIDTH), F32), jax.ShapeDtypeStruct((n, MEM_WIDTH), F32)]
    out_specs = [pl.BlockSpec((tm, MIX_WIDTH), lambda i: (i, 0)), pl.BlockSpec((tm, MEM_WIDTH), lambda i: (i, 0))]
    scratch = []
    if emit_v:
        out_shape.append(jax.ShapeDtypeStruct((n, MIX_WIDTH), F32))
        out_specs.append(pl.BlockSpec((tm, MIX_WIDTH), lambda i: (i, 0)))
    else:
        scratch.append(pltpu.VMEM((tm, MIX_WIDTH), F32))
    return pl.pallas_call(
        functools.partial(_gmlp_in_kernel, seq_len=seq_len, cm=cm),
        grid=(n // tm,),
        in_specs=[
            pl.BlockSpec((tm, d), lambda i: (i, 0)),
            pl.BlockSpec((1, d), const),
            pl.BlockSpec((d, d_in), const),
            pl.BlockSpec((1, MIX_WIDTH), const),
            pl.BlockSpec((GM_GROUPS, cm, cm), lambda i: (0, 0, 0)),
            pl.BlockSpec((cm, MIX_WIDTH), const),
        ],
        out_specs=out_specs,
        out_shape=out_shape,
        scratch_shapes=scratch,
        compiler_params=_cparams(("arbitrary",)),
        name="gmlp_in",
    )(x, gain.reshape(1, d), w_bf, v_gain.reshape(1, MIX_WIDTH), wmix, bias_full)


def _sb_in_rows_kernel(x_ref, gain_ref, w_ref, q_ref, k_ref, v_ref, mq_ref):
    h = _rms(x_ref[...], gain_ref[...]).astype(BF16)
    proj = _dot(h, w_ref[...])
    q_ref[...] = (proj[:, :MIX_WIDTH] * SB_Q_SCALE).astype(BF16)
    k_ref[...] = proj[:, MIX_WIDTH:2 * MIX_WIDTH]
    v_ref[...] = proj[:, 2 * MIX_WIDTH:3 * MIX_WIDTH]
    mq_ref[...] = proj[:, 3 * MIX_WIDTH:]


def _sb_in_rows(x, gain, w_bf):
    n, d = x.shape
    full = lambda width: pl.BlockSpec((n, width), lambda i: (0, 0))
    shp = lambda width, dt: jax.ShapeDtypeStruct((n, width), dt)
    return pl.pallas_call(
        _sb_in_rows_kernel,
        grid=(1,),
        in_specs=[full(d), pl.BlockSpec((1, d), lambda i: (0, 0)), pl.BlockSpec(w_bf.shape, lambda i: (0, 0))],
        out_specs=[full(MIX_WIDTH)] * 3 + [full(MEM_WIDTH)],
        out_shape=[shp(MIX_WIDTH, BF16), shp(MIX_WIDTH, F32), shp(MIX_WIDTH, F32), shp(MEM_WIDTH, F32)],
        compiler_params=_cparams(("arbitrary",)),
        name="sb_in_rows",
    )(x, gain.reshape(1, d), w_bf)


def _sb_in_prompt_kernel(x_ref, gain_ref, wq_ref, wkvt_ref, q_ref, kt_ref, vt_ref, ktb_ref, vtb_ref, mq_ref, *, tk):
    tm = x_ref.shape[1]
    h = _rms(x_ref[0], gain_ref[...]).astype(BF16)
    qm = _dot(h, wq_ref[...])
    q_ref[0] = (qm[:, :MIX_WIDTH] * SB_Q_SCALE).astype(BF16)
    mq_ref[0] = qm[:, MIX_WIDTH:]
    kvt = _nt_dot(wkvt_ref[...], h)
    kt_ref[0] = kvt[:MIX_WIDTH]
    vt_ref[0] = kvt[MIX_WIDTH:]
    for c in range(tm // tk):
        ktb_ref[0, c] = kvt[:MIX_WIDTH, c * tk:(c + 1) * tk].astype(BF16)
        vtb_ref[0, c] = kvt[MIX_WIDTH:, c * tk:(c + 1) * tk].astype(BF16)


def _sb_in_prompt(x, gain, wq_bf, wkvt_bf, *, tm, tk):
    b, t, d = x.shape
    const = lambda bi, i: (0, 0)
    return pl.pallas_call(
        functools.partial(_sb_in_prompt_kernel, tk=tk),
        grid=(b, t // tm),
        in_specs=[
            pl.BlockSpec((1, tm, d), lambda bi, i: (bi, i, 0)),
            pl.BlockSpec((1, d), const),
            pl.BlockSpec((d, MIX_WIDTH + MEM_WIDTH), const),
            pl.BlockSpec((2 * MIX_WIDTH, d), const),
        ],
        out_specs=[
            pl.BlockSpec((1, tm, MIX_WIDTH), lambda bi, i: (bi, i, 0)),
            pl.BlockSpec((1, MIX_WIDTH, tm), lambda bi, i: (bi, 0, i)),
            pl.BlockSpec((1, MIX_WIDTH, tm), lambda bi, i: (bi, 0, i)),
            pl.BlockSpec((1, tm // tk, MIX_WIDTH, tk), lambda bi, i: (bi, i, 0, 0)),
            pl.BlockSpec((1, tm // tk, MIX_WIDTH, tk), lambda bi, i: (bi, i, 0, 0)),
            pl.BlockSpec((1, tm, MEM_WIDTH), lambda bi, i: (bi, i, 0)),
        ],
        out_shape=[
            jax.ShapeDtypeStruct((b, t, MIX_WIDTH), BF16),
            jax.ShapeDtypeStruct((b, MIX_WIDTH, t), F32),
            jax.ShapeDtypeStruct((b, MIX_WIDTH, t), F32),
            jax.ShapeDtypeStruct((b, t // tk, MIX_WIDTH, tk), BF16),
            jax.ShapeDtypeStruct((b, t // tk, MIX_WIDTH, tk), BF16),
            jax.ShapeDtypeStruct((b, t, MEM_WIDTH), F32),
        ],
        compiler_params=_cparams(("arbitrary", "arbitrary")),
        name="sb_in_prompt",
    )(x, gain.reshape(1, d), wq_bf, wkvt_bf)


def _sb_logits(q_stack, k, bias_col, mask, feature_major):
    w = (_dot(q_stack, k) if feature_major else _nt_dot(q_stack, k)) + bias_col
    e = jnp.exp2(-jnp.abs(w))
    log_beta = jnp.minimum(w, 0.0) - jnp.log(1.0 + e) * LOG2E
    sp = w - log_beta
    if mask is not None:
        sp = jnp.where(mask, sp, 0.0)
    return log_beta, sp


def _sb_weights(log_beta, sp, v, neg_tri, carry, mask, feature_major):
    tk = neg_tri.shape[0]
    sp_bf = sp.astype(BF16)
    parts = []
    for j in reversed(range(sp.shape[1] // tk)):
        cols = slice(j * tk, (j + 1) * tk)
        log_after = _dot(sp_bf[:, cols], neg_tri) + carry
        parts.append(log_beta[:, cols] + log_after)
        carry = carry - jnp.sum(sp[:, cols], axis=1, keepdims=True)
    a = jnp.exp2(parts[0] if len(parts) == 1 else jnp.concatenate(parts[::-1], axis=1))
    if mask is not None:
        a = jnp.where(mask, a, 0.0)
    a = a.astype(BF16)
    contrib = _nt_dot(a, v) if feature_major else _dot(a, v)
    return contrib, carry


def _sb_block(q_stack, k, v, bias_col, neg_tri, carry, mask, feature_major):
    log_beta, sp = _sb_logits(q_stack, k, bias_col, mask, feature_major)
    return _sb_weights(log_beta, sp, v, neg_tri, carry, mask, feature_major)


def _neg_tri(tk):
    j = lax.broadcasted_iota(jnp.int32, (tk, tk), 0)
    s = lax.broadcasted_iota(jnp.int32, (tk, tk), 1)
    return jnp.where(j > s, -1.0, 0.0).astype(BF16)


HEADS_PER_GROUP = 4
GROUP_WIDTH = HEADS_PER_GROUP * SB_HEAD_DIM


def _sb_prompt_kernel(bias_ref, q_ref, k_ref, v_ref, o_ref, qs_ref, acc_ref, carry_ref, *, tq):
    hg = pl.program_id(1)
    qi = pl.program_id(2)
    rows = HEADS_PER_GROUP * tq
    qs_ref[...] = _head_stack(q_ref[0], HEADS_PER_GROUP, SB_HEAD_DIM)
    bias_col = jnp.concatenate(
        [jnp.full((tq, 1), bias_ref[hg * HEADS_PER_GROUP + h] * LOG2E, F32) for h in range(HEADS_PER_GROUP)], axis=0)
    tri = _neg_tri(tq)

    def key_run(kj, n_blocks, diagonal):
        k = jnp.concatenate([k_ref[0, kj + i] for i in range(n_blocks)], axis=1)
        v = jnp.concatenate([v_ref[0, kj + i] for i in range(n_blocks)], axis=1)
        if diagonal:
            t_row = lax.broadcasted_iota(jnp.int32, (rows, tq), 0) % tq
            mask = lax.broadcasted_iota(jnp.int32, (rows, tq), 1) < t_row
            carry = jnp.zeros((rows, 1), F32)
        else:
            mask = None
            carry = carry_ref[...]
        contrib, carry = _sb_block(qs_ref[...], k, v, bias_col, tri, carry, mask, True)
        if diagonal:
            acc_ref[...] = contrib
        else:
            acc_ref[...] += contrib
        carry_ref[...] = carry

    key_run(qi, 1, True)

    @pl.when(qi % 2 == 1)
    def _():
        key_run(qi - 1, 1, False)

    n_pairs = qi // 2

    def body(i, _):
        key_run(2 * (n_pairs - 1 - i), 2, False)
        return 0

    lax.fori_loop(0, n_pairs, body, 0)
    o_ref[0] = _head_unstack(acc_ref[...], HEADS_PER_GROUP, SB_HEAD_DIM)


def _sb_prompt(q_bf, kt_bf, vt_bf, bias, *, tq):
    b, t, w = q_bf.shape
    nkb = t // tq
    return pl.pallas_call(
        functools.partial(_sb_prompt_kernel, tq=tq),
        grid=(b, w // GROUP_WIDTH, nkb),
        in_specs=[
            pl.BlockSpec(memory_space=pltpu.SMEM),
            pl.BlockSpec((1, tq, GROUP_WIDTH), lambda bi, g, qi: (bi, qi, g)),
            pl.BlockSpec((1, nkb, GROUP_WIDTH, tq), lambda bi, g, qi: (bi, 0, g, 0)),
            pl.BlockSpec((1, nkb, GROUP_WIDTH, tq), lambda bi, g, qi: (bi, 0, g, 0)),
        ],
        out_specs=pl.BlockSpec((1, tq, GROUP_WIDTH), lambda bi, g, qi: (bi, qi, g)),
        out_shape=jax.ShapeDtypeStruct((b, t, w), F32),
        scratch_shapes=[pltpu.VMEM((HEADS_PER_GROUP * tq, GROUP_WIDTH), BF16),
                        pltpu.VMEM((HEADS_PER_GROUP * tq, GROUP_WIDTH), F32),
                        pltpu.VMEM((HEADS_PER_GROUP * tq, 1), F32)],
        compiler_params=_cparams(("arbitrary", "arbitrary", "arbitrary")),
        name="sb_prompt",
    )(bias, q_bf, kt_bf, vt_bf)


PAGES_PER_STEP = 8


def _sb_decode_kernel(pt_ref, bias_ref, q_ref, kn_ref, vn_ref, *refs, t_new):
    k_pages = refs[:PAGES_PER_STEP]
    v_pages = refs[PAGES_PER_STEP:2 * PAGES_PER_STEP]
    o_ref, acc_ref, carry_ref = refs[2 * PAGES_PER_STEP:]
    j = pl.program_id(1)
    rows = SB_HEADS * t_new
    q_stack = _head_stack(q_ref[0], SB_HEADS, SB_HEAD_DIM)
    bias_col = jnp.concatenate([jnp.full((t_new, 1), bias_ref[h] * LOG2E, F32) for h in range(SB_HEADS)], axis=0)

    @pl.when(j == 0)
    def _():
        pad = jnp.zeros((PAGE_SIZE - t_new, MIX_WIDTH), F32)
        k = jnp.concatenate([kn_ref[0], pad], axis=0).astype(BF16)
        v = jnp.concatenate([vn_ref[0], pad], axis=0).astype(BF16)
        t_row = lax.broadcasted_iota(jnp.int32, (rows, PAGE_SIZE), 0) % t_new
        s_col = lax.broadcasted_iota(jnp.int32, (rows, PAGE_SIZE), 1)
        contrib, carry = _sb_block(q_stack, k, v, bias_col, _neg_tri(PAGE_SIZE),
                                   jnp.zeros((rows, 1), F32), s_col < t_row, False)
        acc_ref[...] = contrib
        carry_ref[...] = carry

    k = jnp.concatenate([r[0].astype(BF16) for r in k_pages], axis=1)
    v = jnp.concatenate([r[0].astype(BF16) for r in v_pages], axis=1)
    contrib, carry = _sb_block(q_stack, k, v, bias_col, _neg_tri(2 * PAGE_SIZE), carry_ref[...], None, True)
    acc_ref[...] += contrib
    carry_ref[...] = carry

    @pl.when(j == pl.num_programs(1) - 1)
    def _():
        o_ref[0] = _head_unstack(acc_ref[...], SB_HEADS, SB_HEAD_DIM)


def _sb_decode(q_bf, k_new, v_new, cache_kt, cache_vt, page_table, bias):
    b, t_new, w = q_bf.shape
    n_pages = page_table.shape[1]
    n_steps = n_pages // PAGES_PER_STEP

    def page_spec(slot):
        def index_map(bi, j, pt, bias):
            return (pt[bi, (n_steps - 1 - j) * PAGES_PER_STEP + slot], 0, 0)
        return pl.BlockSpec((1, w, PAGE_SIZE), index_map)

    new_spec = pl.BlockSpec((1, t_new, w), lambda bi, j, pt, bias: (bi, 0, 0))
    grid_spec = pltpu.PrefetchScalarGridSpec(
        num_scalar_prefetch=2,
        grid=(b, n_steps),
        in_specs=[new_spec, new_spec, new_spec]
        + [page_spec(s) for s in range(PAGES_PER_STEP)] * 2,
        out_specs=new_spec,
        scratch_shapes=[pltpu.VMEM((SB_HEADS * t_new, w), F32), pltpu.VMEM((SB_HEADS * t_new, 1), F32)],
    )
    return pl.pallas_call(
        functools.partial(_sb_decode_kernel, t_new=t_new),
        grid_spec=grid_spec,
        out_shape=jax.ShapeDtypeStruct((b, t_new, w), F32),
        compiler_params=_cparams(("arbitrary", "arbitrary")),
        name="sb_decode",
    )(page_table, bias, q_bf, k_new, v_new, *([cache_kt] * PAGES_PER_STEP), *([cache_vt] * PAGES_PER_STEP))


def _mem_attend_kernel(q_ref, kt_ref, vt_ref, gain_ref, o_ref):
    q = q_ref[0]
    hi, lo = _split(q * q)
    same = _same_head(MEM_WIDTH, MEM_HEAD_DIM)
    ss = _dot(hi, same) + _dot(lo, same)
    qn = q * lax.rsqrt(ss * (1.0 / MEM_HEAD_DIM) + EPS) * gain_ref[...]
    q_stack = _head_stack(qn * (MEM_HEAD_DIM ** -0.5), MEM_HEADS, MEM_HEAD_DIM)
    s = _dot(q_stack, kt_ref[0].astype(BF16))
    p = jnp.exp(s - jnp.max(s, axis=-1, keepdims=True))
    l = jnp.sum(p, axis=-1, keepdims=True)
    o = _nt_dot(p.astype(BF16), vt_ref[0].astype(BF16)) / l
    o_ref[0] = _head_unstack(o, MEM_HEADS, MEM_HEAD_DIM)


def _mem_attend(q, mkt, mvt, q_gain_full, *, tm):
    b, t, w = q.shape
    return pl.pallas_call(
        _mem_attend_kernel,
        grid=(b, t // tm),
        in_specs=[
            pl.BlockSpec((1, tm, w), lambda bi, i: (bi, i, 0)),
            pl.BlockSpec((1, w, MEM_TOKENS), lambda bi, i: (bi, 0, 0)),
            pl.BlockSpec((1, w, MEM_TOKENS), lambda bi, i: (bi, 0, 0)),
            pl.BlockSpec((1, w), lambda bi, i: (0, 0)),
        ],
        out_specs=pl.BlockSpec((1, tm, w), lambda bi, i: (bi, i, 0)),
        out_shape=jax.ShapeDtypeStruct((b, t, w), F32),
        compiler_params=_cparams(("arbitrary", "arbitrary")),
        name="mem_attend",
    )(q, mkt, mvt, q_gain_full.reshape(1, w))


def _out_proj_kernel(mix_ref, mem_ref, x_ref, w_ref, y_ref):
    y = _dot(mix_ref[...].astype(BF16), w_ref[:MIX_WIDTH, :])
    y = y + _dot(mem_ref[...].astype(BF16), w_ref[MIX_WIDTH:, :])
    y_ref[...] = x_ref[...] + y


def _out_proj(mix, mem_out, x, w_bf, *, tm):
    n, d = x.shape
    return pl.pallas_call(
        _out_proj_kernel,
        grid=(n // tm,),
        in_specs=[
            pl.BlockSpec((tm, MIX_WIDTH), lambda i: (i, 0)),
            pl.BlockSpec((tm, MEM_WIDTH), lambda i: (i, 0)),
            pl.BlockSpec((tm, d), lambda i: (i, 0)),
            pl.BlockSpec((d, d), lambda i: (0, 0)),
        ],
        out_specs=pl.BlockSpec((tm, d), lambda i: (i, 0)),
        out_shape=jax.ShapeDtypeStruct((n, d), F32),
        compiler_params=_cparams(("arbitrary",)),
        name="out_proj",
    )(mix, mem_out, x, w_bf)


FF_CHUNK = 256
EXT_PAD = 8


def _conv_ffn_kernel(x_ref, gain_ref, wup_ref, cw_ref, cb_ref, wdown_ref, past_ref,
                     y_ref, state_ref, ext_ref, h_ref, acc_ref):
    t = pl.program_id(1)
    tm = x_ref.shape[1]
    lo = EXT_PAD - (CONV_W - 1)

    @pl.when(t == 0)
    def _():
        ext_ref[lo:EXT_PAD, :] = past_ref[0]

    @pl.when(t > 0)
    def _():
        ext_ref[lo:EXT_PAD, :] = ext_ref[tm + lo:tm + EXT_PAD, :]

    x = x_ref[0]
    h_ref[...] = _rms(x, gain_ref[...]).astype(BF16)
    acc_ref[...] = x
    for c in range(D_FF // FF_CHUNK):
        conv = []
        for half in range(2):
            cols = slice(half * D_FF + c * FF_CHUNK, half * D_FF + (c + 1) * FF_CHUNK)
            up = _dot(h_ref[...], wup_ref[:, cols])
            ext_ref[EXT_PAD:EXT_PAD + tm, cols] = up
            y = cb_ref[:, cols] + cw_ref[CONV_W - 1:CONV_W, cols] * up
            for j in range(CONV_W - 1):
                y = y + cw_ref[j:j + 1, cols] * ext_ref[lo + j:lo + j + tm, cols]
            conv.append(y)
        g = (jax.nn.silu(conv[0]) * conv[1]).astype(BF16)
        acc_ref[...] += _dot(g, wdown_ref[c * FF_CHUNK:(c + 1) * FF_CHUNK, :])
    y_ref[0] = acc_ref[...]
    state_ref[0] = ext_ref[tm + lo:tm + EXT_PAD, :]


def _conv_ffn(x, gain, wup_bf, cw, cb, wdown_bf, past, *, tm):
    b, t, d = x.shape
    ff2 = 2 * D_FF
    const = lambda bi, i: (0, 0)
    return pl.pallas_call(
        _conv_ffn_kernel,
        grid=(b, t // tm),
        in_specs=[
            pl.BlockSpec((1, tm, d), lambda bi, i: (bi, i, 0)),
            pl.BlockSpec((1, d), const),
            pl.BlockSpec((d, ff2), const, pipeline_mode=pl.Buffered(1)),
            pl.BlockSpec((CONV_W, ff2), const),
            pl.BlockSpec((1, ff2), const),
            pl.BlockSpec((D_FF, d), const, pipeline_mode=pl.Buffered(1)),
            pl.BlockSpec((1, CONV_W - 1, ff2), lambda bi, i: (bi, 0, 0)),
        ],
        out_specs=[
            pl.BlockSpec((1, tm, d), lambda bi, i: (bi, i, 0)),
            pl.BlockSpec((1, CONV_W - 1, ff2), lambda bi, i: (bi, 0, 0)),
        ],
        out_shape=[jax.ShapeDtypeStruct((b, t, d), F32), jax.ShapeDtypeStruct((b, CONV_W - 1, ff2), F32)],
        scratch_shapes=[pltpu.VMEM((EXT_PAD + tm, ff2), F32), pltpu.VMEM((tm, d), BF16), pltpu.VMEM((tm, d), F32)],
        compiler_params=_cparams(("arbitrary", "arbitrary")),
        name="conv_ffn",
    )(x, gain.reshape(1, d), wup_bf, cw, cb.reshape(1, ff2), wdown_bf, past)


def _conv_ffn_rows_kernel(x_ref, gain_ref, wup_ref, cw_ref, cb_ref, wdown_ref, prev1_ref, prev2_ref,
                          y_ref, up_ref, h_ref, acc_ref, *, seq_len):
    n = x_ref.shape[0]
    x = x_ref[...]
    h_ref[...] = _rms(x, gain_ref[...]).astype(BF16)
    acc_ref[...] = x
    t_idx = lax.broadcasted_iota(jnp.int32, (n, FF_CHUNK), 0) % seq_len
    for c in range(D_FF // FF_CHUNK):
        conv = []
        for half in range(2):
            cols = slice(half * D_FF + c * FF_CHUNK, half * D_FF + (c + 1) * FF_CHUNK)
            up = _dot(h_ref[...], wup_ref[:, cols])
            up_ref[:, cols] = up
            back1 = jnp.where(t_idx >= 1, pltpu.roll(up, 1, 0), prev1_ref[:, cols])
            back2 = jnp.where(t_idx >= 2, pltpu.roll(up, 2, 0), prev2_ref[:, cols])
            conv.append(cb_ref[:, cols] + cw_ref[2:3, cols] * up + cw_ref[1:2, cols] * back1
                        + cw_ref[0:1, cols] * back2)
        g = (jax.nn.silu(conv[0]) * conv[1]).astype(BF16)
        acc_ref[...] += _dot(g, wdown_ref[c * FF_CHUNK:(c + 1) * FF_CHUNK, :])
    y_ref[...] = acc_ref[...]


def _conv_ffn_rows(x, gain, wup_bf, cw, cb, wdown_bf, past, *, seq_len):
    n, d = x.shape
    ff2 = 2 * D_FF
    b = n // seq_len
    prev2 = jnp.pad(past, ((0, 0), (0, seq_len - 2), (0, 0))).reshape(n, ff2)
    prev1 = jnp.pad(past[:, 1:], ((0, 0), (0, seq_len - 1), (0, 0))).reshape(n, ff2)
    tm = 128
    const = lambda i: (0, 0)
    rows = lambda width: pl.BlockSpec((tm, width), lambda i: (i, 0))
    y, up = pl.pallas_call(
        functools.partial(_conv_ffn_rows_kernel, seq_len=seq_len),
        grid=(n // tm,),
        in_specs=[rows(d), pl.BlockSpec((1, d), const),
                  pl.BlockSpec((d, ff2), const, pipeline_mode=pl.Buffered(1)),
                  pl.BlockSpec((CONV_W, ff2), const), pl.BlockSpec((1, ff2), const),
                  pl.BlockSpec((D_FF, d), const, pipeline_mode=pl.Buffered(1)),
                  rows(ff2), rows(ff2)],
        out_specs=[rows(d), rows(ff2)],
        out_shape=[jax.ShapeDtypeStruct((n, d), F32), jax.ShapeDtypeStruct((n, ff2), F32)],
        scratch_shapes=[pltpu.VMEM((tm, d), BF16), pltpu.VMEM((tm, d), F32)],
        compiler_params=_cparams(("arbitrary",)),
        name="conv_ffn_rows",
    )(x, gain.reshape(1, d), wup_bf, cw, cb.reshape(1, ff2), wdown_bf, prev1, prev2)
    return y, up.reshape(b, seq_len, ff2)[:, seq_len - (CONV_W - 1):]


def _feature_major(x):
    lead = x.shape[:-3]
    n, h, dh = x.shape[-3:]
    nd = len(lead)
    return jnp.transpose(x, (*range(nd), nd + 1, nd + 2, nd)).reshape(*lead, h * dh, n)


def _position_major(xt, n_heads):
    lead = xt.shape[:-2]
    w, n = xt.shape[-2:]
    nd = len(lead)
    x4 = xt.reshape(*lead, n_heads, w // n_heads, n)
    return jnp.transpose(x4, (*range(nd), nd + 2, nd, nd + 1))


def kernel(x_prompt, x_sample, cache_mem_k, cache_mem_v, cache_sb_k, cache_sb_v, state_ffn_conv, page_table, mem_prompt, norm_mix, norm_ffn, norm_mem, w_in_a, gm_v_norm, gm_ws, gm_bs, w_in_b, sb_bias, w_mem_kv, mem_q_norm, mem_k_norm, w_out, w_up, conv_w, conv_b, w_down):
    depth = norm_mix.shape[0]
    bp, tp, d = x_prompt.shape
    bs_, ts, _ = x_sample.shape
    n_p, n_s = bp * tp, bs_ * ts
    sb_tile = 256

    k_gain_col = jnp.tile(mem_k_norm, (1, MEM_HEADS)).reshape(depth, MEM_WIDTH, 1)
    mem_kt_p, mem_vt_p = _mem_kv(mem_prompt, norm_mem, jnp.swapaxes(w_mem_kv, 1, 2).astype(BF16), k_gain_col)
    q_gain_full = jnp.tile(mem_q_norm, (1, MEM_HEADS))
    mem_kt_s = _feature_major(cache_mem_k)
    mem_vt_s = _feature_major(cache_mem_v)

    xp = x_prompt.reshape(n_p, d)
    xs = x_sample.reshape(n_s, d)
    sb_k_p, sb_v_p, sb_k_s, sb_v_s, gm_v_s, conv_p, conv_s = [], [], [], [], [], [], []
    for i in range(depth):
        if i % 2 == 0:
            a = i // 2
            w_bf = w_in_a[a].astype(BF16)
            bias_full = jnp.repeat(gm_bs[a].T, GM_GROUP_DIM, axis=1)
            mix_p, mq_p = _gmlp_in(xp, norm_mix[i], w_bf, gm_v_norm[a], gm_ws[a], bias_full,
                                   seq_len=CHUNK, cm=CHUNK, tm=512, emit_v=False)
            reps = n_s // ts
            wmix_s = jnp.tile(gm_ws[a][:, :ts, :ts], (1, reps, reps))
            bias_s = jnp.tile(bias_full[:ts], (reps, 1))
            mix_s, mq_s, v_new = _gmlp_in(xs, norm_mix[i], w_bf, gm_v_norm[a], wmix_s, bias_s,
                                          seq_len=ts, cm=n_s, tm=n_s, emit_v=True)
            gm_v_s.append(v_new.reshape(bs_, ts, MIX_WIDTH))
        else:
            b = i // 2
            w = w_in_b[b]
            wq_bf = jnp.concatenate([w[:, :MIX_WIDTH], w[:, 3 * MIX_WIDTH:]], axis=1).astype(BF16)
            wkvt_bf = w[:, MIX_WIDTH:3 * MIX_WIDTH].T.astype(BF16)
            q_p, kt_p, vt_p, ktb_p, vtb_p, mq_p = _sb_in_prompt(
                xp.reshape(bp, tp, d), norm_mix[i], wq_bf, wkvt_bf, tm=512, tk=sb_tile)
            mq_p = mq_p.reshape(n_p, MEM_WIDTH)
            q_s, k_s, v_s, mq_s = _sb_in_rows(xs, norm_mix[i], w.astype(BF16))
            mix_p = _sb_prompt(q_p, ktb_p, vtb_p, sb_bias[b], tq=sb_tile).reshape(n_p, MIX_WIDTH)
            mix_s = _sb_decode(q_s.reshape(bs_, ts, MIX_WIDTH), k_s.reshape(bs_, ts, MIX_WIDTH),
                               v_s.reshape(bs_, ts, MIX_WIDTH),
                               _feature_major(cache_sb_k[b]), _feature_major(cache_sb_v[b]),
                               page_table, sb_bias[b]).reshape(n_s, MIX_WIDTH)
            sb_k_p.append(_position_major(kt_p, SB_HEADS))
            sb_v_p.append(_position_major(vt_p, SB_HEADS))
            sb_k_s.append(k_s.reshape(bs_, ts, SB_HEADS, SB_HEAD_DIM))
            sb_v_s.append(v_s.reshape(bs_, ts, SB_HEADS, SB_HEAD_DIM))
        mem_out_p = _mem_attend(mq_p.reshape(bp, tp, MEM_WIDTH), mem_kt_p[i], mem_vt_p[i], q_gain_full[i], tm=256)
        mem_out_s = _mem_attend(mq_s.reshape(bs_, ts, MEM_WIDTH), mem_kt_s[i], mem_vt_s[i], q_gain_full[i], tm=ts)
        w_out_bf = w_out[i].astype(BF16)
        xp = _out_proj(mix_p, mem_out_p.reshape(n_p, MEM_WIDTH), xp, w_out_bf, tm=512)
        xs = _out_proj(mix_s, mem_out_s.reshape(n_s, MEM_WIDTH), xs, w_out_bf, tm=n_s)
        w_up_bf = w_up[i].astype(BF16)
        w_down_bf = w_down[i].astype(BF16)
        zero_rows = jnp.zeros((bp, CONV_W - 1, 2 * D_FF), F32)
        xp3, cp = _conv_ffn(xp.reshape(bp, tp, d), norm_ffn[i], w_up_bf, conv_w[i], conv_b[i], w_down_bf,
                            zero_rows, tm=512)
        xs, cs = _conv_ffn_rows(xs, norm_ffn[i], w_up_bf, conv_w[i], conv_b[i], w_down_bf,
                                state_ffn_conv[i], seq_len=ts)
        xp = xp3.reshape(n_p, d)
        conv_p.append(cp)
        conv_s.append(cs)

    return (xp.reshape(bp, tp, d), xs.reshape(bs_, ts, d),
            _position_major(mem_kt_p, MEM_HEADS), _position_major(mem_vt_p, MEM_HEADS),
            jnp.stack(sb_k_p), jnp.stack(sb_v_p), jnp.stack(sb_k_s), jnp.stack(sb_v_s),
            jnp.stack(gm_v_s), jnp.stack(conv_p), jnp.stack(conv_s))
```

```python
import functools

import jax
import jax.numpy as jnp
from jax import lax
from jax.experimental import pallas as pl
from jax.experimental.pallas import tpu as pltpu

D_MODEL = 1024
MIX_WIDTH = 768
MEM_WIDTH = 256
CHUNK = 128
GM_GROUP_DIM = 128
GM_GROUPS = 6
SB_HEAD_DIM = 64
SB_HEADS = 12
MEM_TOKENS = 256
MEM_HEADS = 4
MEM_HEAD_DIM = 64
D_FF = 2816
CONV_W = 3
PAGE_SIZE = 128
EPS = 1e-6

VMEM_LIMIT = 56 * 1024 * 1024
F32 = jnp.float32
BF16 = jnp.bfloat16
LOG2E = 1.4426950408889634
SB_Q_SCALE = SB_HEAD_DIM ** -0.5 * LOG2E


def _cparams(sem):
    return pltpu.CompilerParams(dimension_semantics=sem, vmem_limit_bytes=VMEM_LIMIT)


def _rms(x, gain):
    ms = jnp.mean(x * x, axis=-1, keepdims=True)
    return x * lax.rsqrt(ms + EPS) * gain


def _nt_dot(a, b):
    return lax.dot_general(a, b, (((1,), (1,)), ((), ())), preferred_element_type=F32)


def _dot(a, b):
    return jnp.dot(a, b, preferred_element_type=F32)


def _split(x):
    hi = x.astype(BF16)
    return hi, (x - hi.astype(F32)).astype(BF16)


def _same_head(width, head_dim):
    r = lax.broadcasted_iota(jnp.int32, (width, width), 0) // head_dim
    c = lax.broadcasted_iota(jnp.int32, (width, width), 1) // head_dim
    return (r == c).astype(BF16)


def _head_stack(q, n_heads, head_dim):
    q = q.astype(F32)
    head = lax.broadcasted_iota(jnp.int32, q.shape, 1) // head_dim
    return jnp.concatenate([jnp.where(head == h, q, 0.0) for h in range(n_heads)], axis=0).astype(BF16)


def _head_unstack(o, n_heads, head_dim):
    t = o.shape[0] // n_heads
    head = lax.broadcasted_iota(jnp.int32, (t, o.shape[1]), 1) // head_dim
    out = jnp.zeros((t, o.shape[1]), o.dtype)
    for h in range(n_heads):
        out = out + jnp.where(head == h, o[h * t:(h + 1) * t], 0.0)
    return out


def _mem_kv_kernel(mem_ref, gain_ref, wt_ref, kgain_ref, k_ref, v_ref):
    h = _rms(mem_ref[0], gain_ref[0]).astype(BF16)
    mt = _nt_dot(wt_ref[0], h)
    kt = mt[:MEM_WIDTH]
    hi, lo = _split(kt * kt)
    same = _same_head(MEM_WIDTH, MEM_HEAD_DIM)
    ss = _dot(same, hi) + _dot(same, lo)
    k_ref[0, 0] = kt * lax.rsqrt(ss * (1.0 / MEM_HEAD_DIM) + EPS) * kgain_ref[0]
    v_ref[0, 0] = mt[MEM_WIDTH:]


def _mem_kv(mem, norm_mem, w_kv_t_bf, k_gain_col):
    depth = norm_mem.shape[0]
    batch, m_tok, d = mem.shape
    out = jax.ShapeDtypeStruct((depth, batch, MEM_WIDTH, m_tok), F32)
    return pl.pallas_call(
        _mem_kv_kernel,
        grid=(depth, batch),
        in_specs=[
            pl.BlockSpec((1, m_tok, d), lambda i, b: (b, 0, 0)),
            pl.BlockSpec((1, 1, d), lambda i, b: (i, 0, 0)),
            pl.BlockSpec((1, 2 * MEM_WIDTH, d), lambda i, b: (i, 0, 0)),
            pl.BlockSpec((1, MEM_WIDTH, 1), lambda i, b: (i, 0, 0)),
        ],
        out_specs=[
            pl.BlockSpec((1, 1, MEM_WIDTH, m_tok), lambda i, b: (i, b, 0, 0)),
            pl.BlockSpec((1, 1, MEM_WIDTH, m_tok), lambda i, b: (i, b, 0, 0)),
        ],
        out_shape=[out, out],
        compiler_params=_cparams(("arbitrary", "arbitrary")),
        name="mem_kv",
    )(mem, norm_mem.reshape(depth, 1, d), w_kv_t_bf, k_gain_col)


def _gmlp_in_kernel(x_ref, gain_ref, w_ref, vgain_ref, ws_ref, bst_ref, mix_ref, mq_ref, v_ref, *, seq_len, cm):
    tm = x_ref.shape[0]
    h = _rms(x_ref[...], gain_ref[...]).astype(BF16)
    proj = _dot(h, w_ref[...])
    mq_ref[...] = proj[:, 2 * MIX_WIDTH:]
    v_ref[...] = _rms(jax.nn.gelu(proj[:, MIX_WIDTH:2 * MIX_WIDTH]), vgain_ref[...])
    mix_ref[...] = jax.nn.gelu(proj[:, :MIX_WIDTH])
    r = lax.broadcasted_iota(jnp.int32, (cm, cm), 0)
    c = lax.broadcasted_iota(jnp.int32, (cm, cm), 1)
    allowed = (r // seq_len == c // seq_len) & (c % seq_len <= r % seq_len)
    if seq_len != cm:
        pick = (lax.broadcasted_iota(jnp.int32, (cm, CHUNK), 1)
                == lax.broadcasted_iota(jnp.int32, (cm, CHUNK), 0) % seq_len).astype(BF16)
    for g in range(GM_GROUPS):
        cols = slice(g * GM_GROUP_DIM, (g + 1) * GM_GROUP_DIM)
        wg = ws_ref[g].astype(BF16)
        if seq_len != cm:
            wg = _nt_dot(_dot(pick, wg).astype(BF16), pick)
        wg = jnp.where(allowed, wg, 0.0).astype(BF16)
        bias_col = jnp.concatenate([bst_ref[:seq_len, g:g + 1]] * (cm // seq_len), axis=0)
        for ci in range(tm // cm):
            rows = slice(ci * cm, (ci + 1) * cm)
            mixed = _dot(wg, v_ref[rows, cols].astype(BF16)) + bias_col
            mix_ref[rows, cols] = mix_ref[rows, cols] * mixed


def _gmlp_in(x, gain, w_bf, v_gain, ws, bs_t, *, seq_len, cm, tm, emit_v):
    n, d = x.shape
    d_in = w_bf.shape[1]
    const = lambda i: (0, 0)
    out_shape = [jax.ShapeDtypeStruct((n, MIX_WIDTH), F32), jax.ShapeDtypeStruct((n, MEM_WIDTH), F32)]
    out_specs = [pl.BlockSpec((tm, MIX_WIDTH), lambda i: (i, 0)), pl.BlockSpec((tm, MEM_WIDTH), lambda i: (i, 0))]
    scratch = []
    if emit_v:
        out_shape.append(jax.ShapeDtypeStruct((n, MIX_WIDTH), F32))
        out_specs.append(pl.BlockSpec((tm, MIX_WIDTH), lambda i: (i, 0)))
    else:
        scratch.append(pltpu.VMEM((tm, MIX_WIDTH), F32))
    return pl.pallas_call(
        functools.partial(_gmlp_in_kernel, seq_len=seq_len, cm=cm),
        grid=(n // tm,),
        in_specs=[
            pl.BlockSpec((tm, d), lambda i: (i, 0)),
            pl.BlockSpec((1, d), const),
            pl.BlockSpec((d, d_in), const),
            pl.BlockSpec((1, MIX_WIDTH), const),
            pl.BlockSpec((GM_GROUPS, CHUNK, CHUNK), lambda i: (0, 0, 0)),
            pl.BlockSpec((CHUNK, GM_GROUPS), const),
        ],
        out_specs=out_specs,
        out_shape=out_shape,
        scratch_shapes=scratch,
        compiler_params=_cparams(("arbitrary",)),
        name="gmlp_in",
    )(x, gain.reshape(1, d), w_bf, v_gain.reshape(1, MIX_WIDTH), ws, bs_t)


def _sb_in_rows_kernel(x_ref, gain_ref, w_ref, q_ref, k_ref, v_ref, mq_ref):
    h = _rms(x_ref[...], gain_ref[...]).astype(BF16)
    proj = _dot(h, w_ref[...])
    q_ref[...] = (proj[:, :MIX_WIDTH] * SB_Q_SCALE).astype(BF16)
    k_ref[...] = proj[:, MIX_WIDTH:2 * MIX_WIDTH]
    v_ref[...] = proj[:, 2 * MIX_WIDTH:3 * MIX_WIDTH]
    mq_ref[...] = proj[:, 3 * MIX_WIDTH:]


def _sb_in_rows(x, gain, w_bf):
    n, d = x.shape
    full = lambda width: pl.BlockSpec((n, width), lambda i: (0, 0))
    shp = lambda width, dt: jax.ShapeDtypeStruct((n, width), dt)
    return pl.pallas_call(
        _sb_in_rows_kernel,
        grid=(1,),
        in_specs=[full(d), pl.BlockSpec((1, d), lambda i: (0, 0)), pl.BlockSpec(w_bf.shape, lambda i: (0, 0))],
        out_specs=[full(MIX_WIDTH)] * 3 + [full(MEM_WIDTH)],
        out_shape=[shp(MIX_WIDTH, BF16), shp(MIX_WIDTH, F32), shp(MIX_WIDTH, F32), shp(MEM_WIDTH, F32)],
        compiler_params=_cparams(("arbitrary",)),
        name="sb_in_rows",
    )(x, gain.reshape(1, d), w_bf)


def _sb_in_prompt_kernel(x_ref, gain_ref, wq_ref, wkvt_ref, q_ref, kt_ref, vt_ref, ktb_ref, vtb_ref, mq_ref, *, tk):
    tm = x_ref.shape[1]
    h = _rms(x_ref[0], gain_ref[...]).astype(BF16)
    qm = _dot(h, wq_ref[...])
    q_ref[0] = (qm[:, :MIX_WIDTH] * SB_Q_SCALE).astype(BF16)
    mq_ref[0] = qm[:, MIX_WIDTH:]
    kvt = _nt_dot(wkvt_ref[...], h)
    kt_ref[0] = kvt[:MIX_WIDTH]
    vt_ref[0] = kvt[MIX_WIDTH:]
    for c in range(tm // tk):
        ktb_ref[0, c] = kvt[:MIX_WIDTH, c * tk:(c + 1) * tk].astype(BF16)
        vtb_ref[0, c] = kvt[MIX_WIDTH:, c * tk:(c + 1) * tk].astype(BF16)


def _sb_in_prompt(x, gain, wq_bf, wkvt_bf, *, tm, tk):
    b, t, d = x.shape
    const = lambda bi, i: (0, 0)
    return pl.pallas_call(
        functools.partial(_sb_in_prompt_kernel, tk=tk),
        grid=(b, t // tm),
        in_specs=[
            pl.BlockSpec((1, tm, d), lambda bi, i: (bi, i, 0)),
            pl.BlockSpec((1, d), const),
            pl.BlockSpec((d, MIX_WIDTH + MEM_WIDTH), const),
            pl.BlockSpec((2 * MIX_WIDTH, d), const),
        ],
        out_specs=[
            pl.BlockSpec((1, tm, MIX_WIDTH), lambda bi, i: (bi, i, 0)),
            pl.BlockSpec((1, MIX_WIDTH, tm), lambda bi, i: (bi, 0, i)),
            pl.BlockSpec((1, MIX_WIDTH, tm), lambda bi, i: (bi, 0, i)),
            pl.BlockSpec((1, tm // tk, MIX_WIDTH, tk), lambda bi, i: (bi, i, 0, 0)),
            pl.BlockSpec((1, tm // tk, MIX_WIDTH, tk), lambda bi, i: (bi, i, 0, 0)),
            pl.BlockSpec((1, tm, MEM_WIDTH), lambda bi, i: (bi, i, 0)),
        ],
        out_shape=[
            jax.ShapeDtypeStruct((b, t, MIX_WIDTH), BF16),
            jax.ShapeDtypeStruct((b, MIX_WIDTH, t), F32),
            jax.ShapeDtypeStruct((b, MIX_WIDTH, t), F32),
            jax.ShapeDtypeStruct((b, t // tk, MIX_WIDTH, tk), BF16),
            jax.ShapeDtypeStruct((b, t // tk, MIX_WIDTH, tk), BF16),
            jax.ShapeDtypeStruct((b, t, MEM_WIDTH), F32),
        ],
        compiler_params=_cparams(("arbitrary", "arbitrary")),
        name="sb_in_prompt",
    )(x, gain.reshape(1, d), wq_bf, wkvt_bf)


def _sb_block(q_stack, k, v, bias_col, neg_tri, carry, mask, feature_major):
    tk = neg_tri.shape[0]
    w = (_dot(q_stack, k) if feature_major else _nt_dot(q_stack, k)) + bias_col
    e = jnp.exp2(-jnp.abs(w))
    log_beta = jnp.minimum(w, 0.0) - jnp.log(1.0 + e) * LOG2E
    sp = w - log_beta
    if mask is not None:
        sp = jnp.where(mask, sp, 0.0)
    sp_bf = sp.astype(BF16)
    parts = []
    for j in reversed(range(sp.shape[1] // tk)):
        cols = slice(j * tk, (j + 1) * tk)
        log_after = _dot(sp_bf[:, cols], neg_tri) + carry
        parts.append(log_beta[:, cols] + log_after)
        carry = carry - jnp.sum(sp[:, cols], axis=1, keepdims=True)
    a = jnp.exp2(parts[0] if len(parts) == 1 else jnp.concatenate(parts[::-1], axis=1))
    if mask is not None:
        a = jnp.where(mask, a, 0.0)
    a = a.astype(BF16)
    contrib = _nt_dot(a, v) if feature_major else _dot(a, v)
    return contrib, carry


def _neg_tri(tk):
    j = lax.broadcasted_iota(jnp.int32, (tk, tk), 0)
    s = lax.broadcasted_iota(jnp.int32, (tk, tk), 1)
    return jnp.where(j > s, -1.0, 0.0).astype(BF16)


HEADS_PER_GROUP = 4
GROUP_WIDTH = HEADS_PER_GROUP * SB_HEAD_DIM


def _sb_prompt_kernel(bias_ref, q_ref, k_ref, v_ref, o_ref, qs_ref, acc_ref, carry_ref, *, tq):
    hg = pl.program_id(1)
    qi = pl.program_id(2)
    rows = HEADS_PER_GROUP * tq
    qs_ref[...] = _head_stack(q_ref[0], HEADS_PER_GROUP, SB_HEAD_DIM)
    bias_col = jnp.concatenate(
        [jnp.full((tq, 1), bias_ref[hg * HEADS_PER_GROUP + h] * LOG2E, F32) for h in range(HEADS_PER_GROUP)], axis=0)
    tri = _neg_tri(tq)

    def key_run(kj, n_blocks, diagonal):
        k = jnp.concatenate([k_ref[0, kj + i] for i in range(n_blocks)], axis=1)
        v = jnp.concatenate([v_ref[0, kj + i] for i in range(n_blocks)], axis=1)
        if diagonal:
            t_row = lax.broadcasted_iota(jnp.int32, (rows, tq), 0) % tq
            mask = lax.broadcasted_iota(jnp.int32, (rows, tq), 1) < t_row
            carry = jnp.zeros((rows, 1), F32)
        else:
            mask = None
            carry = carry_ref[...]
        contrib, carry = _sb_block(qs_ref[...], k, v, bias_col, tri, carry, mask, True)
        if diagonal:
            acc_ref[...] = contrib
        else:
            acc_ref[...] += contrib
        carry_ref[...] = carry

    key_run(qi, 1, True)

    @pl.when(qi % 2 == 1)
    def _():
        key_run(qi - 1, 1, False)

    n_pairs = qi // 2

    def body(i, _):
        key_run(2 * (n_pairs - 1 - i), 2, False)
        return 0

    lax.fori_loop(0, n_pairs, body, 0)
    o_ref[0] = _head_unstack(acc_ref[...], HEADS_PER_GROUP, SB_HEAD_DIM)


def _sb_prompt(q_bf, kt_bf, vt_bf, bias, *, tq):
    b, t, w = q_bf.shape
    nkb = t // tq
    return pl.pallas_call(
        functools.partial(_sb_prompt_kernel, tq=tq),
        grid=(b, w // GROUP_WIDTH, nkb),
        in_specs=[
            pl.BlockSpec(memory_space=pltpu.SMEM),
            pl.BlockSpec((1, tq, GROUP_WIDTH), lambda bi, g, qi: (bi, qi, g)),
            pl.BlockSpec((1, nkb, GROUP_WIDTH, tq), lambda bi, g, qi: (bi, 0, g, 0)),
            pl.BlockSpec((1, nkb, GROUP_WIDTH, tq), lambda bi, g, qi: (bi, 0, g, 0)),
        ],
        out_specs=pl.BlockSpec((1, tq, GROUP_WIDTH), lambda bi, g, qi: (bi, qi, g)),
        out_shape=jax.ShapeDtypeStruct((b, t, w), F32),
        scratch_shapes=[pltpu.VMEM((HEADS_PER_GROUP * tq, GROUP_WIDTH), BF16),
                        pltpu.VMEM((HEADS_PER_GROUP * tq, GROUP_WIDTH), F32),
                        pltpu.VMEM((HEADS_PER_GROUP * tq, 1), F32)],
        compiler_params=_cparams(("arbitrary", "arbitrary", "arbitrary")),
        name="sb_prompt",
    )(bias, q_bf, kt_bf, vt_bf)


PAGES_PER_STEP = 8


def _sb_decode_kernel(pt_ref, bias_ref, q_ref, kn_ref, vn_ref, *refs, t_new):
    k_pages = refs[:PAGES_PER_STEP]
    v_pages = refs[PAGES_PER_STEP:2 * PAGES_PER_STEP]
    o_ref, acc_ref, carry_ref = refs[2 * PAGES_PER_STEP:]
    j = pl.program_id(1)
    rows = SB_HEADS * t_new
    q_stack = _head_stack(q_ref[0], SB_HEADS, SB_HEAD_DIM)
    bias_col = jnp.concatenate([jnp.full((t_new, 1), bias_ref[h] * LOG2E, F32) for h in range(SB_HEADS)], axis=0)

    @pl.when(j == 0)
    def _():
        pad = jnp.zeros((PAGE_SIZE - t_new, MIX_WIDTH), F32)
        k = jnp.concatenate([kn_ref[0], pad], axis=0).astype(BF16)
        v = jnp.concatenate([vn_ref[0], pad], axis=0).astype(BF16)
        t_row = lax.broadcasted_iota(jnp.int32, (rows, PAGE_SIZE), 0) % t_new
        s_col = lax.broadcasted_iota(jnp.int32, (rows, PAGE_SIZE), 1)
        contrib, carry = _sb_block(q_stack, k, v, bias_col, _neg_tri(PAGE_SIZE),
                                   jnp.zeros((rows, 1), F32), s_col < t_row, False)
        acc_ref[...] = contrib
        carry_ref[...] = carry

    k = jnp.concatenate([r[0].astype(BF16) for r in k_pages], axis=1)
    v = jnp.concatenate([r[0].astype(BF16) for r in v_pages], axis=1)
    contrib, carry = _sb_block(q_stack, k, v, bias_col, _neg_tri(2 * PAGE_SIZE), carry_ref[...], None, True)
    acc_ref[...] += contrib
    carry_ref[...] = carry

    @pl.when(j == pl.num_programs(1) - 1)
    def _():
        o_ref[0] = _head_unstack(acc_ref[...], SB_HEADS, SB_HEAD_DIM)


def _sb_decode(q_bf, k_new, v_new, cache_kt, cache_vt, page_table, bias):
    b, t_new, w = q_bf.shape
    n_pages = page_table.shape[1]
    n_steps = n_pages // PAGES_PER_STEP

    def page_spec(slot):
        def index_map(bi, j, pt, bias):
            return (pt[bi, (n_steps - 1 - j) * PAGES_PER_STEP + slot], 0, 0)
        return pl.BlockSpec((1, w, PAGE_SIZE), index_map)

    new_spec = pl.BlockSpec((1, t_new, w), lambda bi, j, pt, bias: (bi, 0, 0))
    grid_spec = pltpu.PrefetchScalarGridSpec(
        num_scalar_prefetch=2,
        grid=(b, n_steps),
        in_specs=[new_spec, new_spec, new_spec]
        + [page_spec(s) for s in range(PAGES_PER_STEP)] * 2,
        out_specs=new_spec,
        scratch_shapes=[pltpu.VMEM((SB_HEADS * t_new, w), F32), pltpu.VMEM((SB_HEADS * t_new, 1), F32)],
    )
    return pl.pallas_call(
        functools.partial(_sb_decode_kernel, t_new=t_new),
        grid_spec=grid_spec,
        out_shape=jax.ShapeDtypeStruct((b, t_new, w), F32),
        compiler_params=_cparams(("arbitrary", "arbitrary")),
        name="sb_decode",
    )(page_table, bias, q_bf, k_new, v_new, *([cache_kt] * PAGES_PER_STEP), *([cache_vt] * PAGES_PER_STEP))


def _mem_attend_kernel(q_ref, kt_ref, vt_ref, gain_ref, o_ref):
    q = q_ref[0]
    hi, lo = _split(q * q)
    same = _same_head(MEM_WIDTH, MEM_HEAD_DIM)
    ss = _dot(hi, same) + _dot(lo, same)
    qn = q * lax.rsqrt(ss * (1.0 / MEM_HEAD_DIM) + EPS) * gain_ref[...]
    q_stack = _head_stack(qn * (MEM_HEAD_DIM ** -0.5), MEM_HEADS, MEM_HEAD_DIM)
    s = _dot(q_stack, kt_ref[0].astype(BF16))
    p = jnp.exp(s - jnp.max(s, axis=-1, keepdims=True))
    l = jnp.sum(p, axis=-1, keepdims=True)
    o = _nt_dot(p.astype(BF16), vt_ref[0].astype(BF16)) / l
    o_ref[0] = _head_unstack(o, MEM_HEADS, MEM_HEAD_DIM)


def _mem_attend(q, mkt, mvt, q_gain_full, *, tm):
    b, t, w = q.shape
    return pl.pallas_call(
        _mem_attend_kernel,
        grid=(b, t // tm),
        in_specs=[
            pl.BlockSpec((1, tm, w), lambda bi, i: (bi, i, 0)),
            pl.BlockSpec((1, w, MEM_TOKENS), lambda bi, i: (bi, 0, 0)),
            pl.BlockSpec((1, w, MEM_TOKENS), lambda bi, i: (bi, 0, 0)),
            pl.BlockSpec((1, w), lambda bi, i: (0, 0)),
        ],
        out_specs=pl.BlockSpec((1, tm, w), lambda bi, i: (bi, i, 0)),
        out_shape=jax.ShapeDtypeStruct((b, t, w), F32),
        compiler_params=_cparams(("arbitrary", "arbitrary")),
        name="mem_attend",
    )(q, mkt, mvt, q_gain_full.reshape(1, w))


def _out_proj_kernel(mix_ref, mem_ref, x_ref, w_ref, y_ref):
    y = _dot(mix_ref[...].astype(BF16), w_ref[:MIX_WIDTH, :])
    y = y + _dot(mem_ref[...].astype(BF16), w_ref[MIX_WIDTH:, :])
    y_ref[...] = x_ref[...] + y


def _out_proj(mix, mem_out, x, w_bf, *, layer, tm):
    n, d = x.shape
    return pl.pallas_call(
        _out_proj_kernel,
        grid=(n // tm,),
        in_specs=[
            pl.BlockSpec((tm, MIX_WIDTH), lambda i: (i, 0)),
            pl.BlockSpec((tm, MEM_WIDTH), lambda i: (i, 0)),
            pl.BlockSpec((tm, d), lambda i: (i, 0)),
            pl.BlockSpec((None, d, d), lambda i: (layer, 0, 0)),
        ],
        out_specs=pl.BlockSpec((tm, d), lambda i: (i, 0)),
        out_shape=jax.ShapeDtypeStruct((n, d), F32),
        compiler_params=_cparams(("arbitrary",)),
        name="out_proj",
    )(mix, mem_out, x, w_bf)


FF_CHUNK = 256
EXT_PAD = 8
UP_AHEAD = 11


def _conv_ffn_kernel(x_ref, gain_ref, wup_ref, cw_ref, cb_ref, wdown_ref, past_ref,
                     y_ref, state_ref, ext_ref, h_ref, acc_ref):
    t = pl.program_id(1)
    tm = x_ref.shape[1]
    lo = EXT_PAD - (CONV_W - 1)

    @pl.when(t == 0)
    def _():
        ext_ref[lo:EXT_PAD, :] = past_ref[0]

    @pl.when(t > 0)
    def _():
        ext_ref[lo:EXT_PAD, :] = ext_ref[tm + lo:tm + EXT_PAD, :]

    x = x_ref[0]
    h_ref[...] = _rms(x, gain_ref[...]).astype(BF16)
    acc_ref[...] = x
    n_chunks = D_FF // FF_CHUNK

    def chunk_cols(c):
        return [slice(half * D_FF + c * FF_CHUNK, half * D_FF + (c + 1) * FF_CHUNK) for half in range(2)]

    def up_project(c):
        for cols in chunk_cols(c):
            ext_ref[EXT_PAD:EXT_PAD + tm, cols] = _dot(h_ref[...], wup_ref[:, cols])

    for c in range(min(UP_AHEAD, n_chunks)):
        up_project(c)
    for c in range(n_chunks):
        if c + UP_AHEAD < n_chunks:
            up_project(c + UP_AHEAD)
        conv = []
        for cols in chunk_cols(c):
            y = cb_ref[:, cols]
            for j in range(CONV_W):
                y = y + cw_ref[j:j + 1, cols] * ext_ref[lo + j:lo + j + tm, cols]
            conv.append(y)
        g = (jax.nn.silu(conv[0]) * conv[1]).astype(BF16)
        acc_ref[...] += _dot(g, wdown_ref[c * FF_CHUNK:(c + 1) * FF_CHUNK, :])
    y_ref[0] = acc_ref[...]
    state_ref[0] = ext_ref[tm + lo:tm + EXT_PAD, :]


def _conv_ffn(x, gain, wup_bf, cw, cb, wdown_bf, past, *, layer, tm):
    b, t, d = x.shape
    ff2 = 2 * D_FF
    const = lambda bi, i: (0, 0)
    return pl.pallas_call(
        _conv_ffn_kernel,
        grid=(b, t // tm),
        in_specs=[
            pl.BlockSpec((1, tm, d), lambda bi, i: (bi, i, 0)),
            pl.BlockSpec((1, d), const),
            pl.BlockSpec((None, d, ff2), lambda bi, i: (layer, 0, 0), pipeline_mode=pl.Buffered(1)),
            pl.BlockSpec((CONV_W, ff2), const),
            pl.BlockSpec((1, ff2), const),
            pl.BlockSpec((None, D_FF, d), lambda bi, i: (layer, 0, 0), pipeline_mode=pl.Buffered(1)),
            pl.BlockSpec((1, CONV_W - 1, ff2), lambda bi, i: (bi, 0, 0)),
        ],
        out_specs=[
            pl.BlockSpec((1, tm, d), lambda bi, i: (bi, i, 0)),
            pl.BlockSpec((1, CONV_W - 1, ff2), lambda bi, i: (bi, 0, 0)),
        ],
        out_shape=[jax.ShapeDtypeStruct((b, t, d), F32), jax.ShapeDtypeStruct((b, CONV_W - 1, ff2), F32)],
        scratch_shapes=[pltpu.VMEM((EXT_PAD + tm, ff2), F32), pltpu.VMEM((tm, d), BF16), pltpu.VMEM((tm, d), F32)],
        compiler_params=_cparams(("arbitrary", "arbitrary")),
        name="conv_ffn",
    )(x, gain.reshape(1, d), wup_bf, cw, cb.reshape(1, ff2), wdown_bf, past)


def _conv_ffn_rows_kernel(x_ref, gain_ref, wup_ref, cw_ref, cb_ref, wdown_ref, prev1_ref, prev2_ref,
                          y_ref, up_ref, h_ref, acc_ref, *, seq_len):
    n = x_ref.shape[0]
    x = x_ref[...]
    h_ref[...] = _rms(x, gain_ref[...]).astype(BF16)
    acc_ref[...] = x
    t_idx = lax.broadcasted_iota(jnp.int32, (n, FF_CHUNK), 0) % seq_len
    n_chunks = D_FF // FF_CHUNK
    for c in range(n_chunks):
        for half in range(2):
            cols = slice(half * D_FF + c * FF_CHUNK, half * D_FF + (c + 1) * FF_CHUNK)
            up_ref[:, cols] = _dot(h_ref[...], wup_ref[:, cols])
    for c in range(n_chunks):
        conv = []
        for half in range(2):
            cols = slice(half * D_FF + c * FF_CHUNK, half * D_FF + (c + 1) * FF_CHUNK)
            up = up_ref[:, cols]
            back1 = jnp.where(t_idx >= 1, pltpu.roll(up, 1, 0), prev1_ref[:, cols])
            back2 = jnp.where(t_idx >= 2, pltpu.roll(up, 2, 0), prev2_ref[:, cols])
            conv.append(cb_ref[:, cols] + cw_ref[2:3, cols] * up + cw_ref[1:2, cols] * back1
                        + cw_ref[0:1, cols] * back2)
        g = (jax.nn.silu(conv[0]) * conv[1]).astype(BF16)
        acc_ref[...] += _dot(g, wdown_ref[c * FF_CHUNK:(c + 1) * FF_CHUNK, :])
    y_ref[...] = acc_ref[...]


def _conv_ffn_rows(x, gain, wup_bf, cw, cb, wdown_bf, past, *, layer, seq_len):
    n, d = x.shape
    ff2 = 2 * D_FF
    b = n // seq_len
    prev2 = jnp.pad(past, ((0, 0), (0, seq_len - 2), (0, 0))).reshape(n, ff2)
    prev1 = jnp.pad(past[:, 1:], ((0, 0), (0, seq_len - 1), (0, 0))).reshape(n, ff2)
    tm = 128
    const = lambda i: (0, 0)
    rows = lambda width: pl.BlockSpec((tm, width), lambda i: (i, 0))
    y, up = pl.pallas_call(
        functools.partial(_conv_ffn_rows_kernel, seq_len=seq_len),
        grid=(n // tm,),
        in_specs=[rows(d), pl.BlockSpec((1, d), const),
                  pl.BlockSpec((None, d, ff2), lambda i: (layer, 0, 0), pipeline_mode=pl.Buffered(1)),
                  pl.BlockSpec((CONV_W, ff2), const), pl.BlockSpec((1, ff2), const),
                  pl.BlockSpec((None, D_FF, d), lambda i: (layer, 0, 0), pipeline_mode=pl.Buffered(1)),
                  rows(ff2), rows(ff2)],
        out_specs=[rows(d), rows(ff2)],
        out_shape=[jax.ShapeDtypeStruct((n, d), F32), jax.ShapeDtypeStruct((n, ff2), F32)],
        scratch_shapes=[pltpu.VMEM((tm, d), BF16), pltpu.VMEM((tm, d), F32)],
        compiler_params=_cparams(("arbitrary",)),
        name="conv_ffn_rows",
    )(x, gain.reshape(1, d), wup_bf, cw, cb.reshape(1, ff2), wdown_bf, prev1, prev2)
    return y, up.reshape(b, seq_len, ff2)[:, seq_len - (CONV_W - 1):]


def _feature_major(x):
    lead = x.shape[:-3]
    n, h, dh = x.shape[-3:]
    nd = len(lead)
    return jnp.transpose(x, (*range(nd), nd + 1, nd + 2, nd)).reshape(*lead, h * dh, n)


def _position_major(xt, n_heads):
    lead = xt.shape[:-2]
    w, n = xt.shape[-2:]
    nd = len(lead)
    x4 = xt.reshape(*lead, n_heads, w // n_heads, n)
    return jnp.transpose(x4, (*range(nd), nd + 2, nd, nd + 1))


def kernel(x_prompt, x_sample, cache_mem_k, cache_mem_v, cache_sb_k, cache_sb_v, state_ffn_conv, page_table, mem_prompt, norm_mix, norm_ffn, norm_mem, w_in_a, gm_v_norm, gm_ws, gm_bs, w_in_b, sb_bias, w_mem_kv, mem_q_norm, mem_k_norm, w_out, w_up, conv_w, conv_b, w_down):
    depth = norm_mix.shape[0]
    bp, tp, d = x_prompt.shape
    bs_, ts, _ = x_sample.shape
    n_p, n_s = bp * tp, bs_ * ts
    sb_tile = 256

    k_gain_col = jnp.tile(mem_k_norm, (1, MEM_HEADS)).reshape(depth, MEM_WIDTH, 1)
    mem_kt_p, mem_vt_p = _mem_kv(mem_prompt, norm_mem, jnp.swapaxes(w_mem_kv, 1, 2).astype(BF16), k_gain_col)
    q_gain_full = jnp.tile(mem_q_norm, (1, MEM_HEADS))
    mem_kt_s = _feature_major(cache_mem_k)
    mem_vt_s = _feature_major(cache_mem_v)

    w_out_bf = w_out.astype(BF16)
    w_up_bf = w_up.astype(BF16)
    w_down_bf = w_down.astype(BF16)

    xp = x_prompt.reshape(n_p, d)
    xs = x_sample.reshape(n_s, d)
    sb_k_p, sb_v_p, sb_k_s, sb_v_s, gm_v_s, conv_p, conv_s = [], [], [], [], [], [], []
    for i in range(depth):
        if i % 2 == 0:
            a = i // 2
            w_bf = w_in_a[a].astype(BF16)
            bs_t = gm_bs[a].T
            mix_p, mq_p = _gmlp_in(xp, norm_mix[i], w_bf, gm_v_norm[a], gm_ws[a], bs_t,
                                   seq_len=CHUNK, cm=CHUNK, tm=512, emit_v=False)
            mix_s, mq_s, v_new = _gmlp_in(xs, norm_mix[i], w_bf, gm_v_norm[a], gm_ws[a], bs_t,
                                          seq_len=ts, cm=n_s, tm=n_s, emit_v=True)
            gm_v_s.append(v_new.reshape(bs_, ts, MIX_WIDTH))
        else:
            b = i // 2
            w = w_in_b[b]
            wq_bf = jnp.concatenate([w[:, :MIX_WIDTH], w[:, 3 * MIX_WIDTH:]], axis=1).astype(BF16)
            wkvt_bf = w[:, MIX_WIDTH:3 * MIX_WIDTH].T.astype(BF16)
            q_p, kt_p, vt_p, ktb_p, vtb_p, mq_p = _sb_in_prompt(
                xp.reshape(bp, tp, d), norm_mix[i], wq_bf, wkvt_bf, tm=512, tk=sb_tile)
            mq_p = mq_p.reshape(n_p, MEM_WIDTH)
            q_s, k_s, v_s, mq_s = _sb_in_rows(xs, norm_mix[i], w.astype(BF16))
            mix_p = _sb_prompt(q_p, ktb_p, vtb_p, sb_bias[b], tq=sb_tile).reshape(n_p, MIX_WIDTH)
            mix_s = _sb_decode(q_s.reshape(bs_, ts, MIX_WIDTH), k_s.reshape(bs_, ts, MIX_WIDTH),
                               v_s.reshape(bs_, ts, MIX_WIDTH),
                               _feature_major(cache_sb_k[b]), _feature_major(cache_sb_v[b]),
                               page_table, sb_bias[b]).reshape(n_s, MIX_WIDTH)
            sb_k_p.append(_position_major(kt_p, SB_HEADS))
            sb_v_p.append(_position_major(vt_p, SB_HEADS))
            sb_k_s.append(k_s.reshape(bs_, ts, SB_HEADS, SB_HEAD_DIM))
            sb_v_s.append(v_s.reshape(bs_, ts, SB_HEADS, SB_HEAD_DIM))
        mem_out_p = _mem_attend(mq_p.reshape(bp, tp, MEM_WIDTH), mem_kt_p[i], mem_vt_p[i], q_gain_full[i], tm=256)
        mem_out_s = _mem_attend(mq_s.reshape(bs_, ts, MEM_WIDTH), mem_kt_s[i], mem_vt_s[i], q_gain_full[i], tm=ts)
        xp = _out_proj(mix_p, mem_out_p.reshape(n_p, MEM_WIDTH), xp, w_out_bf, layer=i, tm=512)
        xs = _out_proj(mix_s, mem_out_s.reshape(n_s, MEM_WIDTH), xs, w_out_bf, layer=i, tm=n_s)
        zero_rows = jnp.zeros((bp, CONV_W - 1, 2 * D_FF), F32)
        xp3, cp = _conv_ffn(xp.reshape(bp, tp, d), norm_ffn[i], w_up_bf, conv_w[i], conv_b[i], w_down_bf,
                            zero_rows, layer=i, tm=512)
        xs, cs = _conv_ffn_rows(xs, norm_ffn[i], w_up_bf, conv_w[i], conv_b[i], w_down_bf,
                                state_ffn_conv[i], layer=i, seq_len=ts)
        xp = xp3.reshape(n_p, d)
        conv_p.append(cp)
        conv_s.append(cs)

    return (xp.reshape(bp, tp, d), xs.reshape(bs_, ts, d),
            _position_major(mem_kt_p, MEM_HEADS), _position_major(mem_vt_p, MEM_HEADS),
            jnp.stack(sb_k_p), jnp.stack(sb_v_p), jnp.stack(sb_k_s), jnp.stack(sb_v_s),
            jnp.stack(gm_v_s), jnp.stack(conv_p), jnp.stack(conv_s))
```

```python
import functools

import jax
import jax.numpy as jnp
from jax import lax
from jax.experimental import pallas as pl
from jax.experimental.pallas import tpu as pltpu

D_MODEL = 1024
MIX_WIDTH = 768
MEM_WIDTH = 256
CHUNK = 128
GM_GROUP_DIM = 128
GM_GROUPS = 6
SB_HEAD_DIM = 64
SB_HEADS = 12
MEM_TOKENS = 256
MEM_HEADS = 4
MEM_HEAD_DIM = 64
D_FF = 2816
CONV_W = 3
PAGE_SIZE = 128
EPS = 1e-6

VMEM_LIMIT = 56 * 1024 * 1024
F32 = jnp.float32
BF16 = jnp.bfloat16
LOG2E = 1.4426950408889634
SB_Q_SCALE = SB_HEAD_DIM ** -0.5 * LOG2E


def _cparams(sem):
    return pltpu.CompilerParams(dimension_semantics=sem, vmem_limit_bytes=VMEM_LIMIT)


def _rms(x, gain):
    ms = jnp.mean(x * x, axis=-1, keepdims=True)
    return x * lax.rsqrt(ms + EPS) * gain


def _nt_dot(a, b):
    return lax.dot_general(a, b, (((1,), (1,)), ((), ())), preferred_element_type=F32)


def _dot(a, b):
    return jnp.dot(a, b, preferred_element_type=F32)


def _split(x):
    hi = x.astype(BF16)
    return hi, (x - hi.astype(F32)).astype(BF16)


def _same_head(width, head_dim):
    r = lax.broadcasted_iota(jnp.int32, (width, width), 0) // head_dim
    c = lax.broadcasted_iota(jnp.int32, (width, width), 1) // head_dim
    return (r == c).astype(BF16)


def _head_stack(q, n_heads, head_dim):
    q = q.astype(F32)
    head = lax.broadcasted_iota(jnp.int32, q.shape, 1) // head_dim
    return jnp.concatenate([jnp.where(head == h, q, 0.0) for h in range(n_heads)], axis=0).astype(BF16)


def _head_unstack(o, n_heads, head_dim):
    t = o.shape[0] // n_heads
    head = lax.broadcasted_iota(jnp.int32, (t, o.shape[1]), 1) // head_dim
    out = jnp.zeros((t, o.shape[1]), o.dtype)
    for h in range(n_heads):
        out = out + jnp.where(head == h, o[h * t:(h + 1) * t], 0.0)
    return out


def _mem_kv_kernel(mem_ref, gain_ref, wt_ref, kgain_ref, k_ref, v_ref):
    h = _rms(mem_ref[0], gain_ref[0]).astype(BF16)
    mt = _nt_dot(wt_ref[0], h)
    kt = mt[:MEM_WIDTH]
    hi, lo = _split(kt * kt)
    same = _same_head(MEM_WIDTH, MEM_HEAD_DIM)
    ss = _dot(same, hi) + _dot(same, lo)
    k_ref[0, 0] = kt * lax.rsqrt(ss * (1.0 / MEM_HEAD_DIM) + EPS) * kgain_ref[0]
    v_ref[0, 0] = mt[MEM_WIDTH:]


def _mem_kv(mem, norm_mem, w_kv_t_bf, k_gain_col):
    depth = norm_mem.shape[0]
    batch, m_tok, d = mem.shape
    out = jax.ShapeDtypeStruct((depth, batch, MEM_WIDTH, m_tok), F32)
    return pl.pallas_call(
        _mem_kv_kernel,
        grid=(depth, batch),
        in_specs=[
            pl.BlockSpec((1, m_tok, d), lambda i, b: (b, 0, 0)),
            pl.BlockSpec((1, 1, d), lambda i, b: (i, 0, 0)),
            pl.BlockSpec((1, 2 * MEM_WIDTH, d), lambda i, b: (i, 0, 0)),
            pl.BlockSpec((1, MEM_WIDTH, 1), lambda i, b: (i, 0, 0)),
        ],
        out_specs=[
            pl.BlockSpec((1, 1, MEM_WIDTH, m_tok), lambda i, b: (i, b, 0, 0)),
            pl.BlockSpec((1, 1, MEM_WIDTH, m_tok), lambda i, b: (i, b, 0, 0)),
        ],
        out_shape=[out, out],
        compiler_params=_cparams(("arbitrary", "arbitrary")),
        name="mem_kv",
    )(mem, norm_mem.reshape(depth, 1, d), w_kv_t_bf, k_gain_col)


def _gmlp_in_kernel(x_ref, gain_ref, w_ref, vgain_ref, ws_ref, bst_ref, mix_ref, mq_ref, v_ref, *, seq_len, cm):
    tm = x_ref.shape[0]
    h = _rms(x_ref[...], gain_ref[...]).astype(BF16)
    proj = _dot(h, w_ref[...])
    mq_ref[...] = proj[:, 2 * MIX_WIDTH:]
    v_ref[...] = _rms(jax.nn.gelu(proj[:, MIX_WIDTH:2 * MIX_WIDTH]), vgain_ref[...])
    mix_ref[...] = jax.nn.gelu(proj[:, :MIX_WIDTH])
    r = lax.broadcasted_iota(jnp.int32, (cm, cm), 0)
    c = lax.broadcasted_iota(jnp.int32, (cm, cm), 1)
    allowed = (r // seq_len == c // seq_len) & (c % seq_len <= r % seq_len)
    if seq_len != cm:
        pick = (lax.broadcasted_iota(jnp.int32, (cm, CHUNK), 1)
                == lax.broadcasted_iota(jnp.int32, (cm, CHUNK), 0) % seq_len).astype(BF16)
    for g in range(GM_GROUPS):
        cols = slice(g * GM_GROUP_DIM, (g + 1) * GM_GROUP_DIM)
        wg = ws_ref[g].astype(BF16)
        if seq_len != cm:
            wg = _nt_dot(_dot(pick, wg).astype(BF16), pick)
        wg = jnp.where(allowed, wg, 0.0).astype(BF16)
        bias_col = jnp.concatenate([bst_ref[:seq_len, g:g + 1]] * (cm // seq_len), axis=0)
        for ci in range(tm // cm):
            rows = slice(ci * cm, (ci + 1) * cm)
            mixed = _dot(wg, v_ref[rows, cols].astype(BF16)) + bias_col
            mix_ref[rows, cols] = mix_ref[rows, cols] * mixed


def _gmlp_in(x, gain, w_bf, v_gain, ws, bs_t, *, seq_len, cm, tm, emit_v):
    n, d = x.shape
    d_in = w_bf.shape[1]
    const = lambda i: (0, 0)
    out_shape = [jax.ShapeDtypeStruct((n, MIX_WIDTH), F32), jax.ShapeDtypeStruct((n, MEM_WIDTH), F32)]
    out_specs = [pl.BlockSpec((tm, MIX_WIDTH), lambda i: (i, 0)), pl.BlockSpec((tm, MEM_WIDTH), lambda i: (i, 0))]
    scratch = []
    if emit_v:
        out_shape.append(jax.ShapeDtypeStruct((n, MIX_WIDTH), F32))
        out_specs.append(pl.BlockSpec((tm, MIX_WIDTH), lambda i: (i, 0)))
    else:
        scratch.append(pltpu.VMEM((tm, MIX_WIDTH), F32))
    return pl.pallas_call(
        functools.partial(_gmlp_in_kernel, seq_len=seq_len, cm=cm),
        grid=(n // tm,),
        in_specs=[
            pl.BlockSpec((tm, d), lambda i: (i, 0)),
            pl.BlockSpec((1, d), const),
            pl.BlockSpec((d, d_in), const),
            pl.BlockSpec((1, MIX_WIDTH), const),
            pl.BlockSpec((GM_GROUPS, CHUNK, CHUNK), lambda i: (0, 0, 0)),
            pl.BlockSpec((CHUNK, GM_GROUPS), const),
        ],
        out_specs=out_specs,
        out_shape=out_shape,
        scratch_shapes=scratch,
        compiler_params=_cparams(("arbitrary",)),
        name="gmlp_in",
    )(x, gain.reshape(1, d), w_bf, v_gain.reshape(1, MIX_WIDTH), ws, bs_t)


def _sb_in_rows_kernel(x_ref, gain_ref, w_ref, q_ref, k_ref, v_ref, mq_ref):
    h = _rms(x_ref[...], gain_ref[...]).astype(BF16)
    proj = _dot(h, w_ref[...])
    q_ref[...] = (proj[:, :MIX_WIDTH] * SB_Q_SCALE).astype(BF16)
    k_ref[...] = proj[:, MIX_WIDTH:2 * MIX_WIDTH]
    v_ref[...] = proj[:, 2 * MIX_WIDTH:3 * MIX_WIDTH]
    mq_ref[...] = proj[:, 3 * MIX_WIDTH:]


def _sb_in_rows(x, gain, w_bf):
    n, d = x.shape
    full = lambda width: pl.BlockSpec((n, width), lambda i: (0, 0))
    shp = lambda width, dt: jax.ShapeDtypeStruct((n, width), dt)
    return pl.pallas_call(
        _sb_in_rows_kernel,
        grid=(1,),
        in_specs=[full(d), pl.BlockSpec((1, d), lambda i: (0, 0)), pl.BlockSpec(w_bf.shape, lambda i: (0, 0))],
        out_specs=[full(MIX_WIDTH)] * 3 + [full(MEM_WIDTH)],
        out_shape=[shp(MIX_WIDTH, BF16), shp(MIX_WIDTH, F32), shp(MIX_WIDTH, F32), shp(MEM_WIDTH, F32)],
        compiler_params=_cparams(("arbitrary",)),
        name="sb_in_rows",
    )(x, gain.reshape(1, d), w_bf)


def _sb_in_prompt_kernel(x_ref, gain_ref, wq_ref, wkvt_ref, q_ref, kt_ref, vt_ref, ktb_ref, vtb_ref, mq_ref, *, tk):
    tm = x_ref.shape[1]
    h = _rms(x_ref[0], gain_ref[...]).astype(BF16)
    qm = _dot(h, wq_ref[...])
    q_ref[0] = (qm[:, :MIX_WIDTH] * SB_Q_SCALE).astype(BF16)
    mq_ref[0] = qm[:, MIX_WIDTH:]
    kvt = _nt_dot(wkvt_ref[...], h)
    kt_ref[0] = kvt[:MIX_WIDTH]
    vt_ref[0] = kvt[MIX_WIDTH:]
    for c in range(tm // tk):
        ktb_ref[0, c] = kvt[:MIX_WIDTH, c * tk:(c + 1) * tk].astype(BF16)
        vtb_ref[0, c] = kvt[MIX_WIDTH:, c * tk:(c + 1) * tk].astype(BF16)


def _sb_in_prompt(x, gain, wq_bf, wkvt_bf, *, tm, tk):
    b, t, d = x.shape
    const = lambda bi, i: (0, 0)
    return pl.pallas_call(
        functools.partial(_sb_in_prompt_kernel, tk=tk),
        grid=(b, t // tm),
        in_specs=[
            pl.BlockSpec((1, tm, d), lambda bi, i: (bi, i, 0)),
            pl.BlockSpec((1, d), const),
            pl.BlockSpec((d, MIX_WIDTH + MEM_WIDTH), const),
            pl.BlockSpec((2 * MIX_WIDTH, d), const),
        ],
        out_specs=[
            pl.BlockSpec((1, tm, MIX_WIDTH), lambda bi, i: (bi, i, 0)),
            pl.BlockSpec((1, MIX_WIDTH, tm), lambda bi, i: (bi, 0, i)),
            pl.BlockSpec((1, MIX_WIDTH, tm), lambda bi, i: (bi, 0, i)),
            pl.BlockSpec((1, tm // tk, MIX_WIDTH, tk), lambda bi, i: (bi, i, 0, 0)),
            pl.BlockSpec((1, tm // tk, MIX_WIDTH, tk), lambda bi, i: (bi, i, 0, 0)),
            pl.BlockSpec((1, tm, MEM_WIDTH), lambda bi, i: (bi, i, 0)),
        ],
        out_shape=[
            jax.ShapeDtypeStruct((b, t, MIX_WIDTH), BF16),
            jax.ShapeDtypeStruct((b, MIX_WIDTH, t), F32),
            jax.ShapeDtypeStruct((b, MIX_WIDTH, t), F32),
            jax.ShapeDtypeStruct((b, t // tk, MIX_WIDTH, tk), BF16),
            jax.ShapeDtypeStruct((b, t // tk, MIX_WIDTH, tk), BF16),
            jax.ShapeDtypeStruct((b, t, MEM_WIDTH), F32),
        ],
        compiler_params=_cparams(("arbitrary", "arbitrary")),
        name="sb_in_prompt",
    )(x, gain.reshape(1, d), wq_bf, wkvt_bf)


def _sb_block(q_stack, k, v, bias_col, neg_tri, carry, mask, feature_major):
    tk = neg_tri.shape[0]
    w = (_dot(q_stack, k) if feature_major else _nt_dot(q_stack, k)) + bias_col
    e = jnp.exp2(-jnp.abs(w))
    log_beta = jnp.minimum(w, 0.0) - jnp.log(1.0 + e) * LOG2E
    sp = w - log_beta
    if mask is not None:
        sp = jnp.where(mask, sp, 0.0)
    sp_bf = sp.astype(BF16)
    parts = []
    for j in reversed(range(sp.shape[1] // tk)):
        cols = slice(j * tk, (j + 1) * tk)
        log_after = _dot(sp_bf[:, cols], neg_tri) + carry
        parts.append(log_beta[:, cols] + log_after)
        carry = carry - jnp.sum(sp[:, cols], axis=1, keepdims=True)
    a = jnp.exp2(parts[0] if len(parts) == 1 else jnp.concatenate(parts[::-1], axis=1))
    if mask is not None:
        a = jnp.where(mask, a, 0.0)
    a = a.astype(BF16)
    contrib = _nt_dot(a, v) if feature_major else _dot(a, v)
    return contrib, carry


def _neg_tri(tk):
    j = lax.broadcasted_iota(jnp.int32, (tk, tk), 0)
    s = lax.broadcasted_iota(jnp.int32, (tk, tk), 1)
    return jnp.where(j > s, -1.0, 0.0).astype(BF16)


HEADS_PER_GROUP = 4
GROUP_WIDTH = HEADS_PER_GROUP * SB_HEAD_DIM


def _sb_prompt_kernel(bias_ref, q_ref, k_ref, v_ref, o_ref, qs_ref, acc_ref, carry_ref, *, tq):
    hg = pl.program_id(1)
    qi = pl.program_id(2)
    rows = HEADS_PER_GROUP * tq
    qs_ref[...] = _head_stack(q_ref[0], HEADS_PER_GROUP, SB_HEAD_DIM)
    bias_col = jnp.concatenate(
        [jnp.full((tq, 1), bias_ref[hg * HEADS_PER_GROUP + h] * LOG2E, F32) for h in range(HEADS_PER_GROUP)], axis=0)
    tri = _neg_tri(tq)

    def key_run(kj, n_blocks, diagonal):
        k = jnp.concatenate([k_ref[0, kj + i] for i in range(n_blocks)], axis=1)
        v = jnp.concatenate([v_ref[0, kj + i] for i in range(n_blocks)], axis=1)
        if diagonal:
            t_row = lax.broadcasted_iota(jnp.int32, (rows, tq), 0) % tq
            mask = lax.broadcasted_iota(jnp.int32, (rows, tq), 1) < t_row
            carry = jnp.zeros((rows, 1), F32)
        else:
            mask = None
            carry = carry_ref[...]
        contrib, carry = _sb_block(qs_ref[...], k, v, bias_col, tri, carry, mask, True)
        if diagonal:
            acc_ref[...] = contrib
        else:
            acc_ref[...] += contrib
        carry_ref[...] = carry

    key_run(qi, 1, True)

    @pl.when(qi % 2 == 1)
    def _():
        key_run(qi - 1, 1, False)

    n_pairs = qi // 2

    def body(i, _):
        key_run(2 * (n_pairs - 1 - i), 2, False)
        return 0

    lax.fori_loop(0, n_pairs, body, 0)
    o_ref[0] = _head_unstack(acc_ref[...], HEADS_PER_GROUP, SB_HEAD_DIM)


def _sb_prompt(q_bf, kt_bf, vt_bf, bias, *, tq):
    b, t, w = q_bf.shape
    nkb = t // tq
    return pl.pallas_call(
        functools.partial(_sb_prompt_kernel, tq=tq),
        grid=(b, w // GROUP_WIDTH, nkb),
        in_specs=[
            pl.BlockSpec(memory_space=pltpu.SMEM),
            pl.BlockSpec((1, tq, GROUP_WIDTH), lambda bi, g, qi: (bi, qi, g)),
            pl.BlockSpec((1, nkb, GROUP_WIDTH, tq), lambda bi, g, qi: (bi, 0, g, 0)),
            pl.BlockSpec((1, nkb, GROUP_WIDTH, tq), lambda bi, g, qi: (bi, 0, g, 0)),
        ],
        out_specs=pl.BlockSpec((1, tq, GROUP_WIDTH), lambda bi, g, qi: (bi, qi, g)),
        out_shape=jax.ShapeDtypeStruct((b, t, w), F32),
        scratch_shapes=[pltpu.VMEM((HEADS_PER_GROUP * tq, GROUP_WIDTH), BF16),
                        pltpu.VMEM((HEADS_PER_GROUP * tq, GROUP_WIDTH), F32),
                        pltpu.VMEM((HEADS_PER_GROUP * tq, 1), F32)],
        compiler_params=_cparams(("arbitrary", "arbitrary", "arbitrary")),
        name="sb_prompt",
    )(bias, q_bf, kt_bf, vt_bf)


PAGES_PER_STEP = 16


def _sb_decode_kernel(pt_ref, bias_ref, q_ref, kn_ref, vn_ref, *refs, t_new):
    k_pages = refs[:PAGES_PER_STEP]
    v_pages = refs[PAGES_PER_STEP:2 * PAGES_PER_STEP]
    o_ref, acc_ref, carry_ref = refs[2 * PAGES_PER_STEP:]
    j = pl.program_id(1)
    rows = SB_HEADS * t_new
    q_stack = _head_stack(q_ref[0], SB_HEADS, SB_HEAD_DIM)
    bias_col = jnp.concatenate([jnp.full((t_new, 1), bias_ref[h] * LOG2E, F32) for h in range(SB_HEADS)], axis=0)

    @pl.when(j == 0)
    def _():
        pad = jnp.zeros((PAGE_SIZE - t_new, MIX_WIDTH), F32)
        k = jnp.concatenate([kn_ref[0], pad], axis=0).astype(BF16)
        v = jnp.concatenate([vn_ref[0], pad], axis=0).astype(BF16)
        t_row = lax.broadcasted_iota(jnp.int32, (rows, PAGE_SIZE), 0) % t_new
        s_col = lax.broadcasted_iota(jnp.int32, (rows, PAGE_SIZE), 1)
        contrib, carry = _sb_block(q_stack, k, v, bias_col, _neg_tri(PAGE_SIZE),
                                   jnp.zeros((rows, 1), F32), s_col < t_row, False)
        acc_ref[...] = contrib
        carry_ref[...] = carry

    k = jnp.concatenate([r[0].astype(BF16) for r in k_pages], axis=1)
    v = jnp.concatenate([r[0].astype(BF16) for r in v_pages], axis=1)
    contrib, carry = _sb_block(q_stack, k, v, bias_col, _neg_tri(2 * PAGE_SIZE), carry_ref[...], None, True)
    acc_ref[...] += contrib
    carry_ref[...] = carry

    @pl.when(j == pl.num_programs(1) - 1)
    def _():
        o_ref[0] = _head_unstack(acc_ref[...], SB_HEADS, SB_HEAD_DIM)


def _sb_decode(q_bf, k_new, v_new, cache_kt, cache_vt, page_table, bias):
    b, t_new, w = q_bf.shape
    n_pages = page_table.shape[1]
    n_steps = n_pages // PAGES_PER_STEP

    def page_spec(slot):
        def index_map(bi, j, pt, bias):
            return (pt[bi, (n_steps - 1 - j) * PAGES_PER_STEP + slot], 0, 0)
        return pl.BlockSpec((1, w, PAGE_SIZE), index_map)

    new_spec = pl.BlockSpec((1, t_new, w), lambda bi, j, pt, bias: (bi, 0, 0))
    grid_spec = pltpu.PrefetchScalarGridSpec(
        num_scalar_prefetch=2,
        grid=(b, n_steps),
        in_specs=[new_spec, new_spec, new_spec]
        + [page_spec(s) for s in range(PAGES_PER_STEP)] * 2,
        out_specs=new_spec,
        scratch_shapes=[pltpu.VMEM((SB_HEADS * t_new, w), F32), pltpu.VMEM((SB_HEADS * t_new, 1), F32)],
    )
    return pl.pallas_call(
        functools.partial(_sb_decode_kernel, t_new=t_new),
        grid_spec=grid_spec,
        out_shape=jax.ShapeDtypeStruct((b, t_new, w), F32),
        compiler_params=_cparams(("arbitrary", "arbitrary")),
        name="sb_decode",
    )(page_table, bias, q_bf, k_new, v_new, *([cache_kt] * PAGES_PER_STEP), *([cache_vt] * PAGES_PER_STEP))


def _mem_attention(q, kt, vt, gain):
    hi, lo = _split(q * q)
    same = _same_head(MEM_WIDTH, MEM_HEAD_DIM)
    ss = _dot(hi, same) + _dot(lo, same)
    qn = q * lax.rsqrt(ss * (1.0 / MEM_HEAD_DIM) + EPS) * gain
    q_stack = _head_stack(qn * (MEM_HEAD_DIM ** -0.5), MEM_HEADS, MEM_HEAD_DIM)
    s = _dot(q_stack, kt.astype(BF16))
    p = jnp.exp(s - jnp.max(s, axis=-1, keepdims=True))
    l = jnp.sum(p, axis=-1, keepdims=True)
    o = _nt_dot(p.astype(BF16), vt.astype(BF16)) / l
    return _head_unstack(o, MEM_HEADS, MEM_HEAD_DIM)


def _mem_attend_kernel(q_ref, kt_ref, vt_ref, gain_ref, o_ref):
    o_ref[0] = _mem_attention(q_ref[0], kt_ref[...], vt_ref[...], gain_ref[...])


def _mem_attend(q, mkt, mvt, q_gain_full, *, layer, tm):
    b, t, w = q.shape
    kv_spec = pl.BlockSpec((None, None, w, MEM_TOKENS), lambda bi, i: (layer, bi, 0, 0))
    return pl.pallas_call(
        _mem_attend_kernel,
        grid=(b, t // tm),
        in_specs=[
            pl.BlockSpec((1, tm, w), lambda bi, i: (bi, i, 0)),
            kv_spec,
            kv_spec,
            pl.BlockSpec((1, w), lambda bi, i: (0, 0)),
        ],
        out_specs=pl.BlockSpec((1, tm, w), lambda bi, i: (bi, i, 0)),
        out_shape=jax.ShapeDtypeStruct((b, t, w), F32),
        compiler_params=_cparams(("arbitrary", "arbitrary")),
        name="mem_attend",
    )(q, mkt, mvt, q_gain_full.reshape(1, w))


def _out_proj_kernel(mix_ref, mem_ref, x_ref, w_ref, y_ref):
    y = _dot(mix_ref[...].astype(BF16), w_ref[:MIX_WIDTH, :])
    y = y + _dot(mem_ref[...].astype(BF16), w_ref[MIX_WIDTH:, :])
    y_ref[...] = x_ref[...] + y


def _out_proj(mix, mem_out, x, w_bf, *, layer, tm):
    n, d = x.shape
    return pl.pallas_call(
        _out_proj_kernel,
        grid=(n // tm,),
        in_specs=[
            pl.BlockSpec((tm, MIX_WIDTH), lambda i: (i, 0)),
            pl.BlockSpec((tm, MEM_WIDTH), lambda i: (i, 0)),
            pl.BlockSpec((tm, d), lambda i: (i, 0)),
            pl.BlockSpec((None, d, d), lambda i: (layer, 0, 0)),
        ],
        out_specs=pl.BlockSpec((tm, d), lambda i: (i, 0)),
        out_shape=jax.ShapeDtypeStruct((n, d), F32),
        compiler_params=_cparams(("arbitrary",)),
        name="out_proj",
    )(mix, mem_out, x, w_bf)


def _mem_out_kernel(mix_ref, mq_ref, kt_ref, vt_ref, gain_ref, x_ref, w_ref, y_ref):
    y = _dot(mix_ref[0].astype(BF16), w_ref[:MIX_WIDTH, :])
    mem = _mem_attention(mq_ref[0], kt_ref[...], vt_ref[...], gain_ref[...])
    y_ref[0] = x_ref[0] + (y + _dot(mem.astype(BF16), w_ref[MIX_WIDTH:, :]))


def _mem_out(mix, mq, mkt, mvt, q_gain_full, x, w_bf, *, layer, tm):
    b, t, d = x.shape
    rows = lambda width: pl.BlockSpec((1, tm, width), lambda bi, i: (bi, i, 0))
    kv_spec = pl.BlockSpec((None, None, MEM_WIDTH, MEM_TOKENS), lambda bi, i: (layer, bi, 0, 0))
    return pl.pallas_call(
        _mem_out_kernel,
        grid=(b, t // tm),
        in_specs=[rows(MIX_WIDTH), rows(MEM_WIDTH), kv_spec, kv_spec,
                  pl.BlockSpec((1, MEM_WIDTH), lambda bi, i: (0, 0)), rows(d),
                  pl.BlockSpec((None, d, d), lambda bi, i: (layer, 0, 0))],
        out_specs=rows(d),
        out_shape=jax.ShapeDtypeStruct((b, t, d), F32),
        compiler_params=_cparams(("arbitrary", "arbitrary")),
        name="mem_out",
    )(mix, mq, mkt, mvt, q_gain_full.reshape(1, MEM_WIDTH), x, w_bf)


FF_CHUNK = 256
EXT_PAD = 8
UP_AHEAD = 11


def _conv_ffn_kernel(x_ref, gain_ref, wup_ref, cw_ref, cb_ref, wdown_ref, past_ref,
                     y_ref, state_ref, ext_ref, h_ref, acc_ref):
    t = pl.program_id(1)
    tm = x_ref.shape[1]
    lo = EXT_PAD - (CONV_W - 1)

    @pl.when(t == 0)
    def _():
        ext_ref[lo:EXT_PAD, :] = past_ref[0]

    @pl.when(t > 0)
    def _():
        ext_ref[lo:EXT_PAD, :] = ext_ref[tm + lo:tm + EXT_PAD, :]

    x = x_ref[0]
    h_ref[...] = _rms(x, gain_ref[...]).astype(BF16)
    acc_ref[...] = x
    n_chunks = D_FF // FF_CHUNK

    def chunk_cols(c):
        return [slice(half * D_FF + c * FF_CHUNK, half * D_FF + (c + 1) * FF_CHUNK) for half in range(2)]

    def up_project(c):
        for cols in chunk_cols(c):
            ext_ref[EXT_PAD:EXT_PAD + tm, cols] = _dot(h_ref[...], wup_ref[:, cols])

    for c in range(min(UP_AHEAD, n_chunks)):
        up_project(c)
    for c in range(n_chunks):
        if c + UP_AHEAD < n_chunks:
            up_project(c + UP_AHEAD)
        conv = []
        for cols in chunk_cols(c):
            y = cb_ref[:, cols]
            for j in range(CONV_W):
                y = y + cw_ref[j:j + 1, cols] * ext_ref[lo + j:lo + j + tm, cols]
            conv.append(y)
        g = (jax.nn.silu(conv[0]) * conv[1]).astype(BF16)
        acc_ref[...] += _dot(g, wdown_ref[c * FF_CHUNK:(c + 1) * FF_CHUNK, :])
    y_ref[0] = acc_ref[...]
    state_ref[0] = ext_ref[tm + lo:tm + EXT_PAD, :]


def _conv_ffn(x, gain, wup_bf, cw, cb, wdown_bf, past, *, layer, tm):
    b, t, d = x.shape
    ff2 = 2 * D_FF
    const = lambda bi, i: (0, 0)
    return pl.pallas_call(
        _conv_ffn_kernel,
        grid=(b, t // tm),
        in_specs=[
            pl.BlockSpec((1, tm, d), lambda bi, i: (bi, i, 0)),
            pl.BlockSpec((1, d), const),
            pl.BlockSpec((None, d, ff2), lambda bi, i: (layer, 0, 0), pipeline_mode=pl.Buffered(1)),
            pl.BlockSpec((CONV_W, ff2), const),
            pl.BlockSpec((1, ff2), const),
            pl.BlockSpec((None, D_FF, d), lambda bi, i: (layer, 0, 0), pipeline_mode=pl.Buffered(1)),
            pl.BlockSpec((1, CONV_W - 1, ff2), lambda bi, i: (bi, 0, 0)),
        ],
        out_specs=[
            pl.BlockSpec((1, tm, d), lambda bi, i: (bi, i, 0)),
            pl.BlockSpec((1, CONV_W - 1, ff2), lambda bi, i: (bi, 0, 0)),
        ],
        out_shape=[jax.ShapeDtypeStruct((b, t, d), F32), jax.ShapeDtypeStruct((b, CONV_W - 1, ff2), F32)],
        scratch_shapes=[pltpu.VMEM((EXT_PAD + tm, ff2), F32), pltpu.VMEM((tm, d), BF16), pltpu.VMEM((tm, d), F32)],
        compiler_params=_cparams(("arbitrary", "arbitrary")),
        name="conv_ffn",
    )(x, gain.reshape(1, d), wup_bf, cw, cb.reshape(1, ff2), wdown_bf, past)


def _conv_ffn_rows_kernel(x_ref, gain_ref, wup_ref, cw_ref, cb_ref, wdown_ref, prev1_ref, prev2_ref,
                          y_ref, up_ref, h_ref, acc_ref, *, seq_len):
    n = x_ref.shape[0]
    x = x_ref[...]
    h_ref[...] = _rms(x, gain_ref[...]).astype(BF16)
    acc_ref[...] = x
    t_idx = lax.broadcasted_iota(jnp.int32, (n, FF_CHUNK), 0) % seq_len
    n_chunks = D_FF // FF_CHUNK
    for c in range(n_chunks):
        for half in range(2):
            cols = slice(half * D_FF + c * FF_CHUNK, half * D_FF + (c + 1) * FF_CHUNK)
            up_ref[:, cols] = _dot(h_ref[...], wup_ref[:, cols])
    for c in range(n_chunks):
        conv = []
        for half in range(2):
            cols = slice(half * D_FF + c * FF_CHUNK, half * D_FF + (c + 1) * FF_CHUNK)
            up = up_ref[:, cols]
            back1 = jnp.where(t_idx >= 1, pltpu.roll(up, 1, 0), prev1_ref[:, cols])
            back2 = jnp.where(t_idx >= 2, pltpu.roll(up, 2, 0), prev2_ref[:, cols])
            conv.append(cb_ref[:, cols] + cw_ref[2:3, cols] * up + cw_ref[1:2, cols] * back1
                        + cw_ref[0:1, cols] * back2)
        g = (jax.nn.silu(conv[0]) * conv[1]).astype(BF16)
        acc_ref[...] += _dot(g, wdown_ref[c * FF_CHUNK:(c + 1) * FF_CHUNK, :])
    y_ref[...] = acc_ref[...]


def _conv_ffn_rows(x, gain, wup_bf, cw, cb, wdown_bf, past, *, layer, seq_len):
    n, d = x.shape
    ff2 = 2 * D_FF
    b = n // seq_len
    prev2 = jnp.pad(past, ((0, 0), (0, seq_len - 2), (0, 0))).reshape(n, ff2)
    prev1 = jnp.pad(past[:, 1:], ((0, 0), (0, seq_len - 1), (0, 0))).reshape(n, ff2)
    tm = 128
    const = lambda i: (0, 0)
    rows = lambda width: pl.BlockSpec((tm, width), lambda i: (i, 0))
    y, up = pl.pallas_call(
        functools.partial(_conv_ffn_rows_kernel, seq_len=seq_len),
        grid=(n // tm,),
        in_specs=[rows(d), pl.BlockSpec((1, d), const),
                  pl.BlockSpec((None, d, ff2), lambda i: (layer, 0, 0), pipeline_mode=pl.Buffered(1)),
                  pl.BlockSpec((CONV_W, ff2), const), pl.BlockSpec((1, ff2), const),
                  pl.BlockSpec((None, D_FF, d), lambda i: (layer, 0, 0), pipeline_mode=pl.Buffered(1)),
                  rows(ff2), rows(ff2)],
        out_specs=[rows(d), rows(ff2)],
        out_shape=[jax.ShapeDtypeStruct((n, d), F32), jax.ShapeDtypeStruct((n, ff2), F32)],
        scratch_shapes=[pltpu.VMEM((tm, d), BF16), pltpu.VMEM((tm, d), F32)],
        compiler_params=_cparams(("arbitrary",)),
        name="conv_ffn_rows",
    )(x, gain.reshape(1, d), wup_bf, cw, cb.reshape(1, ff2), wdown_bf, prev1, prev2)
    return y, up.reshape(b, seq_len, ff2)[:, seq_len - (CONV_W - 1):]


def _feature_major(x):
    lead = x.shape[:-3]
    n, h, dh = x.shape[-3:]
    nd = len(lead)
    return jnp.transpose(x, (*range(nd), nd + 1, nd + 2, nd)).reshape(*lead, h * dh, n)


def _position_major(xt, n_heads):
    lead = xt.shape[:-2]
    w, n = xt.shape[-2:]
    nd = len(lead)
    x4 = xt.reshape(*lead, n_heads, w // n_heads, n)
    return jnp.transpose(x4, (*range(nd), nd + 2, nd, nd + 1))


def kernel(x_prompt, x_sample, cache_mem_k, cache_mem_v, cache_sb_k, cache_sb_v, state_ffn_conv, page_table, mem_prompt, norm_mix, norm_ffn, norm_mem, w_in_a, gm_v_norm, gm_ws, gm_bs, w_in_b, sb_bias, w_mem_kv, mem_q_norm, mem_k_norm, w_out, w_up, conv_w, conv_b, w_down):
    depth = norm_mix.shape[0]
    bp, tp, d = x_prompt.shape
    bs_, ts, _ = x_sample.shape
    n_p, n_s = bp * tp, bs_ * ts
    sb_tile = 256

    k_gain_col = jnp.tile(mem_k_norm, (1, MEM_HEADS)).reshape(depth, MEM_WIDTH, 1)
    mem_kt_p, mem_vt_p = _mem_kv(mem_prompt, norm_mem, jnp.swapaxes(w_mem_kv, 1, 2).astype(BF16), k_gain_col)
    q_gain_full = jnp.tile(mem_q_norm, (1, MEM_HEADS))
    mem_kt_s = _feature_major(cache_mem_k)
    mem_vt_s = _feature_major(cache_mem_v)

    w_out_bf = w_out.astype(BF16)
    w_up_bf = w_up.astype(BF16)
    w_down_bf = w_down.astype(BF16)

    xp = x_prompt.reshape(n_p, d)
    xs = x_sample.reshape(n_s, d)
    sb_k_p, sb_v_p, sb_k_s, sb_v_s, gm_v_s, conv_p, conv_s = [], [], [], [], [], [], []
    for i in range(depth):
        if i % 2 == 0:
            a = i // 2
            w_bf = w_in_a[a].astype(BF16)
            bs_t = gm_bs[a].T
            mix_p, mq_p = _gmlp_in(xp, norm_mix[i], w_bf, gm_v_norm[a], gm_ws[a], bs_t,
                                   seq_len=CHUNK, cm=CHUNK, tm=512, emit_v=False)
            mix_s, mq_s, v_new = _gmlp_in(xs, norm_mix[i], w_bf, gm_v_norm[a], gm_ws[a], bs_t,
                                          seq_len=ts, cm=n_s, tm=n_s, emit_v=True)
            gm_v_s.append(v_new.reshape(bs_, ts, MIX_WIDTH))
        else:
            b = i // 2
            w = w_in_b[b]
            wq_bf = jnp.concatenate([w[:, :MIX_WIDTH], w[:, 3 * MIX_WIDTH:]], axis=1).astype(BF16)
            wkvt_bf = w[:, MIX_WIDTH:3 * MIX_WIDTH].T.astype(BF16)
            q_p, kt_p, vt_p, ktb_p, vtb_p, mq_p = _sb_in_prompt(
                xp.reshape(bp, tp, d), norm_mix[i], wq_bf, wkvt_bf, tm=512, tk=sb_tile)
            mq_p = mq_p.reshape(n_p, MEM_WIDTH)
            q_s, k_s, v_s, mq_s = _sb_in_rows(xs, norm_mix[i], w.astype(BF16))
            mix_p = _sb_prompt(q_p, ktb_p, vtb_p, sb_bias[b], tq=sb_tile).reshape(n_p, MIX_WIDTH)
            mix_s = _sb_decode(q_s.reshape(bs_, ts, MIX_WIDTH), k_s.reshape(bs_, ts, MIX_WIDTH),
                               v_s.reshape(bs_, ts, MIX_WIDTH),
                               _feature_major(cache_sb_k[b]), _feature_major(cache_sb_v[b]),
                               page_table, sb_bias[b]).reshape(n_s, MIX_WIDTH)
            sb_k_p.append(_position_major(kt_p, SB_HEADS))
            sb_v_p.append(_position_major(vt_p, SB_HEADS))
            sb_k_s.append(k_s.reshape(bs_, ts, SB_HEADS, SB_HEAD_DIM))
            sb_v_s.append(v_s.reshape(bs_, ts, SB_HEADS, SB_HEAD_DIM))
        xp3 = _mem_out(mix_p.reshape(bp, tp, MIX_WIDTH), mq_p.reshape(bp, tp, MEM_WIDTH), mem_kt_p, mem_vt_p,
                       q_gain_full[i], xp.reshape(bp, tp, d), w_out_bf, layer=i, tm=512)
        mem_out_s = _mem_attend(mq_s.reshape(bs_, ts, MEM_WIDTH), mem_kt_s, mem_vt_s, q_gain_full[i],
                                layer=i, tm=ts)
        xs = _out_proj(mix_s, mem_out_s.reshape(n_s, MEM_WIDTH), xs, w_out_bf, layer=i, tm=n_s)
        zero_rows = jnp.zeros((bp, CONV_W - 1, 2 * D_FF), F32)
        xp3, cp = _conv_ffn(xp3, norm_ffn[i], w_up_bf, conv_w[i], conv_b[i], w_down_bf,
                            zero_rows, layer=i, tm=512)
        xs, cs = _conv_ffn_rows(xs, norm_ffn[i], w_up_bf, conv_w[i], conv_b[i], w_down_bf,
                                state_ffn_conv[i], layer=i, seq_len=ts)
        xp = xp3.reshape(n_p, d)
        conv_p.append(cp)
        conv_s.append(cs)

    return (xp.reshape(bp, tp, d), xs.reshape(bs_, ts, d),
            _position_major(mem_kt_p, MEM_HEADS), _position_major(mem_vt_p, MEM_HEADS),
            jnp.stack(sb_k_p), jnp.stack(sb_v_p), jnp.stack(sb_k_s), jnp.stack(sb_v_s),
            jnp.stack(gm_v_s), jnp.stack(conv_p), jnp.stack(conv_s))
```

```python
import functools

import jax
import jax.numpy as jnp
from jax import lax
from jax.experimental import pallas as pl
from jax.experimental.pallas import tpu as pltpu

D_MODEL = 1024
MIX_WIDTH = 768
MEM_WIDTH = 256
CHUNK = 128
GM_GROUP_DIM = 128
GM_GROUPS = 6
SB_HEAD_DIM = 64
SB_HEADS = 12
MEM_TOKENS = 256
MEM_HEADS = 4
MEM_HEAD_DIM = 64
D_FF = 2816
CONV_W = 3
PAGE_SIZE = 128
EPS = 1e-6

VMEM_LIMIT = 56 * 1024 * 1024
F32 = jnp.float32
BF16 = jnp.bfloat16
LOG2E = 1.4426950408889634
SB_Q_SCALE = SB_HEAD_DIM ** -0.5 * LOG2E


def _cparams(sem):
    return pltpu.CompilerParams(dimension_semantics=sem, vmem_limit_bytes=VMEM_LIMIT)


def _rms(x, gain):
    ms = jnp.mean(x * x, axis=-1, keepdims=True)
    return x * lax.rsqrt(ms + EPS) * gain


def _nt_dot(a, b):
    return lax.dot_general(a, b, (((1,), (1,)), ((), ())), preferred_element_type=F32)


def _dot(a, b):
    return jnp.dot(a, b, preferred_element_type=F32)


def _split(x):
    hi = x.astype(BF16)
    return hi, (x - hi.astype(F32)).astype(BF16)


def _same_head(width, head_dim):
    r = lax.broadcasted_iota(jnp.int32, (width, width), 0) // head_dim
    c = lax.broadcasted_iota(jnp.int32, (width, width), 1) // head_dim
    return (r == c).astype(BF16)


def _head_stack(q, n_heads, head_dim):
    q = q.astype(F32)
    head = lax.broadcasted_iota(jnp.int32, q.shape, 1) // head_dim
    return jnp.concatenate([jnp.where(head == h, q, 0.0) for h in range(n_heads)], axis=0).astype(BF16)


def _head_unstack(o, n_heads, head_dim):
    t = o.shape[0] // n_heads
    head = lax.broadcasted_iota(jnp.int32, (t, o.shape[1]), 1) // head_dim
    out = jnp.zeros((t, o.shape[1]), o.dtype)
    for h in range(n_heads):
        out = out + jnp.where(head == h, o[h * t:(h + 1) * t], 0.0)
    return out


def _mem_kv_kernel(mem_ref, gain_ref, wt_ref, kgain_ref, k_ref, v_ref):
    h = _rms(mem_ref[0], gain_ref[0]).astype(BF16)
    mt = _nt_dot(wt_ref[0], h)
    kt = mt[:MEM_WIDTH]
    hi, lo = _split(kt * kt)
    same = _same_head(MEM_WIDTH, MEM_HEAD_DIM)
    ss = _dot(same, hi) + _dot(same, lo)
    k_ref[0, 0] = kt * lax.rsqrt(ss * (1.0 / MEM_HEAD_DIM) + EPS) * kgain_ref[0]
    v_ref[0, 0] = mt[MEM_WIDTH:]


def _mem_kv(mem, norm_mem, w_kv_t_bf, k_gain_col):
    depth = norm_mem.shape[0]
    batch, m_tok, d = mem.shape
    out = jax.ShapeDtypeStruct((depth, batch, MEM_WIDTH, m_tok), F32)
    return pl.pallas_call(
        _mem_kv_kernel,
        grid=(depth, batch),
        in_specs=[
            pl.BlockSpec((1, m_tok, d), lambda i, b: (b, 0, 0)),
            pl.BlockSpec((1, 1, d), lambda i, b: (i, 0, 0)),
            pl.BlockSpec((1, 2 * MEM_WIDTH, d), lambda i, b: (i, 0, 0)),
            pl.BlockSpec((1, MEM_WIDTH, 1), lambda i, b: (i, 0, 0)),
        ],
        out_specs=[
            pl.BlockSpec((1, 1, MEM_WIDTH, m_tok), lambda i, b: (i, b, 0, 0)),
            pl.BlockSpec((1, 1, MEM_WIDTH, m_tok), lambda i, b: (i, b, 0, 0)),
        ],
        out_shape=[out, out],
        compiler_params=_cparams(("arbitrary", "arbitrary")),
        name="mem_kv",
    )(mem, norm_mem.reshape(depth, 1, d), w_kv_t_bf, k_gain_col)


def _gmlp_in_kernel(x_ref, gain_ref, w_ref, vgain_ref, ws_ref, bst_ref, mix_ref, mq_ref, v_ref, *, seq_len, cm):
    tm = x_ref.shape[0]
    h = _rms(x_ref[...], gain_ref[...]).astype(BF16)
    proj = _dot(h, w_ref[...])
    mq_ref[...] = proj[:, 2 * MIX_WIDTH:]
    v_ref[...] = _rms(jax.nn.gelu(proj[:, MIX_WIDTH:2 * MIX_WIDTH]), vgain_ref[...])
    mix_ref[...] = jax.nn.gelu(proj[:, :MIX_WIDTH])
    r = lax.broadcasted_iota(jnp.int32, (cm, cm), 0)
    c = lax.broadcasted_iota(jnp.int32, (cm, cm), 1)
    allowed = (r // seq_len == c // seq_len) & (c % seq_len <= r % seq_len)
    if seq_len != cm:
        pick = (lax.broadcasted_iota(jnp.int32, (cm, CHUNK), 1)
                == lax.broadcasted_iota(jnp.int32, (cm, CHUNK), 0) % seq_len).astype(BF16)
    for g in range(GM_GROUPS):
        cols = slice(g * GM_GROUP_DIM, (g + 1) * GM_GROUP_DIM)
        wg = ws_ref[g].astype(BF16)
        if seq_len != cm:
            wg = _nt_dot(_dot(pick, wg).astype(BF16), pick)
        wg = jnp.where(allowed, wg, 0.0).astype(BF16)
        bias_col = jnp.concatenate([bst_ref[:seq_len, g:g + 1]] * (cm // seq_len), axis=0)
        for ci in range(tm // cm):
            rows = slice(ci * cm, (ci + 1) * cm)
            mixed = _dot(wg, v_ref[rows, cols].astype(BF16)) + bias_col
            mix_ref[rows, cols] = mix_ref[rows, cols] * mixed


def _gmlp_in(x, gain, w_bf, v_gain, ws, bs_t, *, seq_len, cm, tm, emit_v):
    n, d = x.shape
    d_in = w_bf.shape[1]
    const = lambda i: (0, 0)
    out_shape = [jax.ShapeDtypeStruct((n, MIX_WIDTH), F32), jax.ShapeDtypeStruct((n, MEM_WIDTH), F32)]
    out_specs = [pl.BlockSpec((tm, MIX_WIDTH), lambda i: (i, 0)), pl.BlockSpec((tm, MEM_WIDTH), lambda i: (i, 0))]
    scratch = []
    if emit_v:
        out_shape.append(jax.ShapeDtypeStruct((n, MIX_WIDTH), F32))
        out_specs.append(pl.BlockSpec((tm, MIX_WIDTH), lambda i: (i, 0)))
    else:
        scratch.append(pltpu.VMEM((tm, MIX_WIDTH), F32))
    return pl.pallas_call(
        functools.partial(_gmlp_in_kernel, seq_len=seq_len, cm=cm),
        grid=(n // tm,),
        in_specs=[
            pl.BlockSpec((tm, d), lambda i: (i, 0)),
            pl.BlockSpec((1, d), const),
            pl.BlockSpec((d, d_in), const),
            pl.BlockSpec((1, MIX_WIDTH), const),
            pl.BlockSpec((GM_GROUPS, CHUNK, CHUNK), lambda i: (0, 0, 0)),
            pl.BlockSpec((CHUNK, GM_GROUPS), const),
        ],
        out_specs=out_specs,
        out_shape=out_shape,
        scratch_shapes=scratch,
        compiler_params=_cparams(("arbitrary",)),
        name="gmlp_in",
    )(x, gain.reshape(1, d), w_bf, v_gain.reshape(1, MIX_WIDTH), ws, bs_t)


def _sb_in_rows_kernel(x_ref, gain_ref, w_ref, q_ref, k_ref, v_ref, mq_ref):
    h = _rms(x_ref[...], gain_ref[...]).astype(BF16)
    proj = _dot(h, w_ref[...])
    q_ref[...] = (proj[:, :MIX_WIDTH] * SB_Q_SCALE).astype(BF16)
    k_ref[...] = proj[:, MIX_WIDTH:2 * MIX_WIDTH]
    v_ref[...] = proj[:, 2 * MIX_WIDTH:3 * MIX_WIDTH]
    mq_ref[...] = proj[:, 3 * MIX_WIDTH:]


def _sb_in_rows(x, gain, w_bf):
    n, d = x.shape
    full = lambda width: pl.BlockSpec((n, width), lambda i: (0, 0))
    shp = lambda width, dt: jax.ShapeDtypeStruct((n, width), dt)
    return pl.pallas_call(
        _sb_in_rows_kernel,
        grid=(1,),
        in_specs=[full(d), pl.BlockSpec((1, d), lambda i: (0, 0)), pl.BlockSpec(w_bf.shape, lambda i: (0, 0))],
        out_specs=[full(MIX_WIDTH)] * 3 + [full(MEM_WIDTH)],
        out_shape=[shp(MIX_WIDTH, BF16), shp(MIX_WIDTH, F32), shp(MIX_WIDTH, F32), shp(MEM_WIDTH, F32)],
        compiler_params=_cparams(("arbitrary",)),
        name="sb_in_rows",
    )(x, gain.reshape(1, d), w_bf)


def _sb_in_prompt_kernel(x_ref, gain_ref, wq_ref, wkvt_ref, q_ref, kt_ref, vt_ref, ktb_ref, vtb_ref, mq_ref, *, tk):
    tm = x_ref.shape[1]
    h = _rms(x_ref[0], gain_ref[...]).astype(BF16)
    qm = _dot(h, wq_ref[...])
    q_ref[0] = (qm[:, :MIX_WIDTH] * SB_Q_SCALE).astype(BF16)
    mq_ref[0] = qm[:, MIX_WIDTH:]
    kvt = _nt_dot(wkvt_ref[...], h)
    kt_ref[0] = kvt[:MIX_WIDTH]
    vt_ref[0] = kvt[MIX_WIDTH:]
    for c in range(tm // tk):
        ktb_ref[0, c] = kvt[:MIX_WIDTH, c * tk:(c + 1) * tk].astype(BF16)
        vtb_ref[0, c] = kvt[MIX_WIDTH:, c * tk:(c + 1) * tk].astype(BF16)


def _sb_in_prompt(x, gain, wq_bf, wkvt_bf, *, tm, tk):
    b, t, d = x.shape
    const = lambda bi, i: (0, 0)
    return pl.pallas_call(
        functools.partial(_sb_in_prompt_kernel, tk=tk),
        grid=(b, t // tm),
        in_specs=[
            pl.BlockSpec((1, tm, d), lambda bi, i: (bi, i, 0)),
            pl.BlockSpec((1, d), const),
            pl.BlockSpec((d, MIX_WIDTH + MEM_WIDTH), const),
            pl.BlockSpec((2 * MIX_WIDTH, d), const),
        ],
        out_specs=[
            pl.BlockSpec((1, tm, MIX_WIDTH), lambda bi, i: (bi, i, 0)),
            pl.BlockSpec((1, MIX_WIDTH, tm), lambda bi, i: (bi, 0, i)),
            pl.BlockSpec((1, MIX_WIDTH, tm), lambda bi, i: (bi, 0, i)),
            pl.BlockSpec((1, tm // tk, MIX_WIDTH, tk), lambda bi, i: (bi, i, 0, 0)),
            pl.BlockSpec((1, tm // tk, MIX_WIDTH, tk), lambda bi, i: (bi, i, 0, 0)),
            pl.BlockSpec((1, tm, MEM_WIDTH), lambda bi, i: (bi, i, 0)),
        ],
        out_shape=[
            jax.ShapeDtypeStruct((b, t, MIX_WIDTH), BF16),
            jax.ShapeDtypeStruct((b, MIX_WIDTH, t), F32),
            jax.ShapeDtypeStruct((b, MIX_WIDTH, t), F32),
            jax.ShapeDtypeStruct((b, t // tk, MIX_WIDTH, tk), BF16),
            jax.ShapeDtypeStruct((b, t // tk, MIX_WIDTH, tk), BF16),
            jax.ShapeDtypeStruct((b, t, MEM_WIDTH), F32),
        ],
        compiler_params=_cparams(("arbitrary", "arbitrary")),
        name="sb_in_prompt",
    )(x, gain.reshape(1, d), wq_bf, wkvt_bf)


def _sb_block(q_stack, k, v, bias_col, neg_tri, carry, mask, feature_major):
    tk = neg_tri.shape[0]
    w = (_dot(q_stack, k) if feature_major else _nt_dot(q_stack, k)) + bias_col
    sp = jnp.maximum(w, 0.0) + jnp.log(1.0 + jnp.exp2(-jnp.abs(w))) * LOG2E
    if mask is not None:
        sp = jnp.where(mask, sp, 0.0)
    sp_bf = sp.astype(BF16)
    parts = []
    for j in reversed(range(sp.shape[1] // tk)):
        cols = slice(j * tk, (j + 1) * tk)
        from_here = _dot(sp_bf[:, cols], neg_tri)
        parts.append(w[:, cols] + (from_here + carry))
        carry = carry + from_here[:, 0:1]
    a = jnp.exp2(parts[0] if len(parts) == 1 else jnp.concatenate(parts[::-1], axis=1))
    if mask is not None:
        a = jnp.where(mask, a, 0.0)
    a = a.astype(BF16)
    contrib = _nt_dot(a, v) if feature_major else _dot(a, v)
    return contrib, carry


def _neg_tri(tk):
    j = lax.broadcasted_iota(jnp.int32, (tk, tk), 0)
    s = lax.broadcasted_iota(jnp.int32, (tk, tk), 1)
    return jnp.where(j >= s, -1.0, 0.0).astype(BF16)


HEADS_PER_GROUP = 4
GROUP_WIDTH = HEADS_PER_GROUP * SB_HEAD_DIM
SB_RUN = 4


def _sb_prompt_kernel(bias_ref, q_ref, k_ref, v_ref, o_ref, qs_ref, acc_ref, carry_ref, *, tq):
    hg = pl.program_id(1)
    qi = pl.program_id(2)
    rows = HEADS_PER_GROUP * tq
    qs_ref[...] = _head_stack(q_ref[0], HEADS_PER_GROUP, SB_HEAD_DIM)
    bias_col = jnp.concatenate(
        [jnp.full((tq, 1), bias_ref[hg * HEADS_PER_GROUP + h] * LOG2E, F32) for h in range(HEADS_PER_GROUP)], axis=0)
    tri = _neg_tri(tq)

    def key_run(kj, n_blocks, diagonal):
        k = jnp.concatenate([k_ref[0, kj + i] for i in range(n_blocks)], axis=1)
        v = jnp.concatenate([v_ref[0, kj + i] for i in range(n_blocks)], axis=1)
        if diagonal:
            t_row = lax.broadcasted_iota(jnp.int32, (rows, tq), 0) % tq
            mask = lax.broadcasted_iota(jnp.int32, (rows, tq), 1) < t_row
            carry = jnp.zeros((rows, 1), F32)
        else:
            mask = None
            carry = carry_ref[...]
        contrib, carry = _sb_block(qs_ref[...], k, v, bias_col, tri, carry, mask, True)
        if diagonal:
            acc_ref[...] = contrib
        else:
            acc_ref[...] += contrib
        carry_ref[...] = carry

    key_run(qi, 1, True)
    n_runs = qi // SB_RUN

    def single(i, _):
        key_run(qi - 1 - i, 1, False)
        return 0

    lax.fori_loop(0, qi - n_runs * SB_RUN, single, 0)

    def run(i, _):
        key_run(SB_RUN * (n_runs - 1 - i), SB_RUN, False)
        return 0

    lax.fori_loop(0, n_runs, run, 0)
    o_ref[0] = _head_unstack(acc_ref[...], HEADS_PER_GROUP, SB_HEAD_DIM)


def _sb_prompt(q_bf, kt_bf, vt_bf, bias, *, tq):
    b, t, w = q_bf.shape
    nkb = t // tq
    return pl.pallas_call(
        functools.partial(_sb_prompt_kernel, tq=tq),
        grid=(b, w // GROUP_WIDTH, nkb),
        in_specs=[
            pl.BlockSpec(memory_space=pltpu.SMEM),
            pl.BlockSpec((1, tq, GROUP_WIDTH), lambda bi, g, qi: (bi, qi, g)),
            pl.BlockSpec((1, nkb, GROUP_WIDTH, tq), lambda bi, g, qi: (bi, 0, g, 0)),
            pl.BlockSpec((1, nkb, GROUP_WIDTH, tq), lambda bi, g, qi: (bi, 0, g, 0)),
        ],
        out_specs=pl.BlockSpec((1, tq, GROUP_WIDTH), lambda bi, g, qi: (bi, qi, g)),
        out_shape=jax.ShapeDtypeStruct((b, t, w), F32),
        scratch_shapes=[pltpu.VMEM((HEADS_PER_GROUP * tq, GROUP_WIDTH), BF16),
                        pltpu.VMEM((HEADS_PER_GROUP * tq, GROUP_WIDTH), F32),
                        pltpu.VMEM((HEADS_PER_GROUP * tq, 1), F32)],
        compiler_params=_cparams(("arbitrary", "arbitrary", "arbitrary")),
        name="sb_prompt",
    )(bias, q_bf, kt_bf, vt_bf)


PAGES_PER_STEP = 16


def _sb_decode_kernel(pt_ref, bias_ref, q_ref, kn_ref, vn_ref, *refs, t_new):
    k_pages = refs[:PAGES_PER_STEP]
    v_pages = refs[PAGES_PER_STEP:2 * PAGES_PER_STEP]
    o_ref, acc_ref, carry_ref = refs[2 * PAGES_PER_STEP:]
    j = pl.program_id(1)
    rows = SB_HEADS * t_new
    q_stack = _head_stack(q_ref[0], SB_HEADS, SB_HEAD_DIM)
    bias_col = jnp.concatenate([jnp.full((t_new, 1), bias_ref[h] * LOG2E, F32) for h in range(SB_HEADS)], axis=0)

    @pl.when(j == 0)
    def _():
        pad = jnp.zeros((PAGE_SIZE - t_new, MIX_WIDTH), F32)
        k = jnp.concatenate([kn_ref[0], pad], axis=0).astype(BF16)
        v = jnp.concatenate([vn_ref[0], pad], axis=0).astype(BF16)
        t_row = lax.broadcasted_iota(jnp.int32, (rows, PAGE_SIZE), 0) % t_new
        s_col = lax.broadcasted_iota(jnp.int32, (rows, PAGE_SIZE), 1)
        contrib, carry = _sb_block(q_stack, k, v, bias_col, _neg_tri(PAGE_SIZE),
                                   jnp.zeros((rows, 1), F32), s_col < t_row, False)
        acc_ref[...] = contrib
        carry_ref[...] = carry

    k = jnp.concatenate([r[0].astype(BF16) for r in k_pages], axis=1)
    v = jnp.concatenate([r[0].astype(BF16) for r in v_pages], axis=1)
    contrib, carry = _sb_block(q_stack, k, v, bias_col, _neg_tri(2 * PAGE_SIZE), carry_ref[...], None, True)
    acc_ref[...] += contrib
    carry_ref[...] = carry

    @pl.when(j == pl.num_programs(1) - 1)
    def _():
        o_ref[0] = _head_unstack(acc_ref[...], SB_HEADS, SB_HEAD_DIM)


def _sb_decode(q_bf, k_new, v_new, cache_kt, cache_vt, page_table, bias):
    b, t_new, w = q_bf.shape
    n_pages = page_table.shape[1]
    n_steps = n_pages // PAGES_PER_STEP

    def page_spec(slot):
        def index_map(bi, j, pt, bias):
            return (pt[bi, (n_steps - 1 - j) * PAGES_PER_STEP + slot], 0, 0)
        return pl.BlockSpec((1, w, PAGE_SIZE), index_map)

    new_spec = pl.BlockSpec((1, t_new, w), lambda bi, j, pt, bias: (bi, 0, 0))
    grid_spec = pltpu.PrefetchScalarGridSpec(
        num_scalar_prefetch=2,
        grid=(b, n_steps),
        in_specs=[new_spec, new_spec, new_spec]
        + [page_spec(s) for s in range(PAGES_PER_STEP)] * 2,
        out_specs=new_spec,
        scratch_shapes=[pltpu.VMEM((SB_HEADS * t_new, w), F32), pltpu.VMEM((SB_HEADS * t_new, 1), F32)],
    )
    return pl.pallas_call(
        functools.partial(_sb_decode_kernel, t_new=t_new),
        grid_spec=grid_spec,
        out_shape=jax.ShapeDtypeStruct((b, t_new, w), F32),
        compiler_params=_cparams(("arbitrary", "arbitrary")),
        name="sb_decode",
    )(page_table, bias, q_bf, k_new, v_new, *([cache_kt] * PAGES_PER_STEP), *([cache_vt] * PAGES_PER_STEP))


def _mem_attention(q, kt, vt, gain):
    hi, lo = _split(q * q)
    same = _same_head(MEM_WIDTH, MEM_HEAD_DIM)
    ss = _dot(hi, same) + _dot(lo, same)
    qn = q * lax.rsqrt(ss * (1.0 / MEM_HEAD_DIM) + EPS) * gain
    q_stack = _head_stack(qn * (MEM_HEAD_DIM ** -0.5), MEM_HEADS, MEM_HEAD_DIM)
    s = _dot(q_stack, kt.astype(BF16))
    p = jnp.exp(s - jnp.max(s, axis=-1, keepdims=True))
    l = jnp.sum(p, axis=-1, keepdims=True)
    o = _nt_dot(p.astype(BF16), vt.astype(BF16)) / l
    return _head_unstack(o, MEM_HEADS, MEM_HEAD_DIM)


def _mem_attend_kernel(q_ref, kt_ref, vt_ref, gain_ref, o_ref):
    o_ref[0] = _mem_attention(q_ref[0], kt_ref[...], vt_ref[...], gain_ref[...])


def _mem_attend(q, mkt, mvt, q_gain_full, *, layer, tm):
    b, t, w = q.shape
    kv_spec = pl.BlockSpec((None, None, w, MEM_TOKENS), lambda bi, i: (layer, bi, 0, 0))
    return pl.pallas_call(
        _mem_attend_kernel,
        grid=(b, t // tm),
        in_specs=[
            pl.BlockSpec((1, tm, w), lambda bi, i: (bi, i, 0)),
            kv_spec,
            kv_spec,
            pl.BlockSpec((1, w), lambda bi, i: (0, 0)),
        ],
        out_specs=pl.BlockSpec((1, tm, w), lambda bi, i: (bi, i, 0)),
        out_shape=jax.ShapeDtypeStruct((b, t, w), F32),
        compiler_params=_cparams(("arbitrary", "arbitrary")),
        name="mem_attend",
    )(q, mkt, mvt, q_gain_full.reshape(1, w))


def _out_proj_kernel(mix_ref, mem_ref, x_ref, w_ref, y_ref):
    y = _dot(mix_ref[...].astype(BF16), w_ref[:MIX_WIDTH, :])
    y = y + _dot(mem_ref[...].astype(BF16), w_ref[MIX_WIDTH:, :])
    y_ref[...] = x_ref[...] + y


def _out_proj(mix, mem_out, x, w_bf, *, layer, tm):
    n, d = x.shape
    return pl.pallas_call(
        _out_proj_kernel,
        grid=(n // tm,),
        in_specs=[
            pl.BlockSpec((tm, MIX_WIDTH), lambda i: (i, 0)),
            pl.BlockSpec((tm, MEM_WIDTH), lambda i: (i, 0)),
            pl.BlockSpec((tm, d), lambda i: (i, 0)),
            pl.BlockSpec((None, d, d), lambda i: (layer, 0, 0)),
        ],
        out_specs=pl.BlockSpec((tm, d), lambda i: (i, 0)),
        out_shape=jax.ShapeDtypeStruct((n, d), F32),
        compiler_params=_cparams(("arbitrary",)),
        name="out_proj",
    )(mix, mem_out, x, w_bf)


def _mem_out_kernel(mix_ref, mq_ref, kt_ref, vt_ref, gain_ref, x_ref, w_ref, y_ref):
    y = _dot(mix_ref[0].astype(BF16), w_ref[:MIX_WIDTH, :])
    mem = _mem_attention(mq_ref[0], kt_ref[...], vt_ref[...], gain_ref[...])
    y_ref[0] = x_ref[0] + (y + _dot(mem.astype(BF16), w_ref[MIX_WIDTH:, :]))


def _mem_out(mix, mq, mkt, mvt, q_gain_full, x, w_bf, *, layer, tm):
    b, t, d = x.shape
    rows = lambda width: pl.BlockSpec((1, tm, width), lambda bi, i: (bi, i, 0))
    kv_spec = pl.BlockSpec((None, None, MEM_WIDTH, MEM_TOKENS), lambda bi, i: (layer, bi, 0, 0))
    return pl.pallas_call(
        _mem_out_kernel,
        grid=(b, t // tm),
        in_specs=[rows(MIX_WIDTH), rows(MEM_WIDTH), kv_spec, kv_spec,
                  pl.BlockSpec((1, MEM_WIDTH), lambda bi, i: (0, 0)), rows(d),
                  pl.BlockSpec((None, d, d), lambda bi, i: (layer, 0, 0))],
        out_specs=rows(d),
        out_shape=jax.ShapeDtypeStruct((b, t, d), F32),
        compiler_params=_cparams(("arbitrary", "arbitrary")),
        name="mem_out",
    )(mix, mq, mkt, mvt, q_gain_full.reshape(1, MEM_WIDTH), x, w_bf)


FF_CHUNK = 256
EXT_PAD = 8
UP_AHEAD = 11


def _conv_ffn_kernel(x_ref, gain_ref, wup_ref, cw_ref, cb_ref, wdown_ref, past_ref,
                     y_ref, state_ref, ext_ref, h_ref, acc_ref):
    t = pl.program_id(1)
    tm = x_ref.shape[1]
    lo = EXT_PAD - (CONV_W - 1)

    @pl.when(t == 0)
    def _():
        ext_ref[lo:EXT_PAD, :] = past_ref[0]

    @pl.when(t > 0)
    def _():
        ext_ref[lo:EXT_PAD, :] = ext_ref[tm + lo:tm + EXT_PAD, :]

    x = x_ref[0]
    h_ref[...] = _rms(x, gain_ref[...]).astype(BF16)
    acc_ref[...] = x
    n_chunks = D_FF // FF_CHUNK

    def chunk_cols(c):
        return [slice(half * D_FF + c * FF_CHUNK, half * D_FF + (c + 1) * FF_CHUNK) for half in range(2)]

    def up_project(c):
        for cols in chunk_cols(c):
            ext_ref[EXT_PAD:EXT_PAD + tm, cols] = _dot(h_ref[...], wup_ref[:, cols])

    for c in range(min(UP_AHEAD, n_chunks)):
        up_project(c)
    for c in range(n_chunks):
        if c + UP_AHEAD < n_chunks:
            up_project(c + UP_AHEAD)
        conv = []
        for cols in chunk_cols(c):
            y = cb_ref[:, cols]
            for j in range(CONV_W):
                y = y + cw_ref[j:j + 1, cols] * ext_ref[lo + j:lo + j + tm, cols]
            conv.append(y)
        g = (jax.nn.silu(conv[0]) * conv[1]).astype(BF16)
        acc_ref[...] += _dot(g, wdown_ref[c * FF_CHUNK:(c + 1) * FF_CHUNK, :])
    y_ref[0] = acc_ref[...]
    state_ref[0] = ext_ref[tm + lo:tm + EXT_PAD, :]


def _conv_ffn(x, gain, wup_bf, cw, cb, wdown_bf, past, *, layer, tm):
    b, t, d = x.shape
    ff2 = 2 * D_FF
    const = lambda bi, i: (0, 0)
    return pl.pallas_call(
        _conv_ffn_kernel,
        grid=(b, t // tm),
        in_specs=[
            pl.BlockSpec((1, tm, d), lambda bi, i: (bi, i, 0)),
            pl.BlockSpec((1, d), const),
            pl.BlockSpec((None, d, ff2), lambda bi, i: (layer, 0, 0), pipeline_mode=pl.Buffered(1)),
            pl.BlockSpec((CONV_W, ff2), const),
            pl.BlockSpec((1, ff2), const),
            pl.BlockSpec((None, D_FF, d), lambda bi, i: (layer, 0, 0), pipeline_mode=pl.Buffered(1)),
            pl.BlockSpec((1, CONV_W - 1, ff2), lambda bi, i: (bi, 0, 0)),
        ],
        out_specs=[
            pl.BlockSpec((1, tm, d), lambda bi, i: (bi, i, 0)),
            pl.BlockSpec((1, CONV_W - 1, ff2), lambda bi, i: (bi, 0, 0)),
        ],
        out_shape=[jax.ShapeDtypeStruct((b, t, d), F32), jax.ShapeDtypeStruct((b, CONV_W - 1, ff2), F32)],
        scratch_shapes=[pltpu.VMEM((EXT_PAD + tm, ff2), F32), pltpu.VMEM((tm, d), BF16), pltpu.VMEM((tm, d), F32)],
        compiler_params=_cparams(("arbitrary", "arbitrary")),
        name="conv_ffn",
    )(x, gain.reshape(1, d), wup_bf, cw, cb.reshape(1, ff2), wdown_bf, past)


def _conv_ffn_rows_kernel(x_ref, gain_ref, wup_ref, cw_ref, cb_ref, wdown_ref, prev1_ref, prev2_ref,
                          y_ref, up_ref, h_ref, acc_ref, *, seq_len):
    n = x_ref.shape[0]
    x = x_ref[...]
    h_ref[...] = _rms(x, gain_ref[...]).astype(BF16)
    acc_ref[...] = x
    t_idx = lax.broadcasted_iota(jnp.int32, (n, FF_CHUNK), 0) % seq_len
    n_chunks = D_FF // FF_CHUNK
    for c in range(n_chunks):
        for half in range(2):
            cols = slice(half * D_FF + c * FF_CHUNK, half * D_FF + (c + 1) * FF_CHUNK)
            up_ref[:, cols] = _dot(h_ref[...], wup_ref[:, cols])
    for c in range(n_chunks):
        conv = []
        for half in range(2):
            cols = slice(half * D_FF + c * FF_CHUNK, half * D_FF + (c + 1) * FF_CHUNK)
            up = up_ref[:, cols]
            back1 = jnp.where(t_idx >= 1, pltpu.roll(up, 1, 0), prev1_ref[:, cols])
            back2 = jnp.where(t_idx >= 2, pltpu.roll(up, 2, 0), prev2_ref[:, cols])
            conv.append(cb_ref[:, cols] + cw_ref[2:3, cols] * up + cw_ref[1:2, cols] * back1
                        + cw_ref[0:1, cols] * back2)
        g = (jax.nn.silu(conv[0]) * conv[1]).astype(BF16)
        acc_ref[...] += _dot(g, wdown_ref[c * FF_CHUNK:(c + 1) * FF_CHUNK, :])
    y_ref[...] = acc_ref[...]


def _conv_ffn_rows(x, gain, wup_bf, cw, cb, wdown_bf, past, *, layer, seq_len):
    n, d = x.shape
    ff2 = 2 * D_FF
    b = n // seq_len
    prev2 = jnp.pad(past, ((0, 0), (0, seq_len - 2), (0, 0))).reshape(n, ff2)
    prev1 = jnp.pad(past[:, 1:], ((0, 0), (0, seq_len - 1), (0, 0))).reshape(n, ff2)
    tm = 128
    const = lambda i: (0, 0)
    rows = lambda width: pl.BlockSpec((tm, width), lambda i: (i, 0))
    y, up = pl.pallas_call(
        functools.partial(_conv_ffn_rows_kernel, seq_len=seq_len),
        grid=(n // tm,),
        in_specs=[rows(d), pl.BlockSpec((1, d), const),
                  pl.BlockSpec((None, d, ff2), lambda i: (layer, 0, 0), pipeline_mode=pl.Buffered(1)),
                  pl.BlockSpec((CONV_W, ff2), const), pl.BlockSpec((1, ff2), const),
                  pl.BlockSpec((None, D_FF, d), lambda i: (layer, 0, 0), pipeline_mode=pl.Buffered(1)),
                  rows(ff2), rows(ff2)],
        out_specs=[rows(d), rows(ff2)],
        out_shape=[jax.ShapeDtypeStruct((n, d), F32), jax.ShapeDtypeStruct((n, ff2), F32)],
        scratch_shapes=[pltpu.VMEM((tm, d), BF16), pltpu.VMEM((tm, d), F32)],
        compiler_params=_cparams(("arbitrary",)),
        name="conv_ffn_rows",
    )(x, gain.reshape(1, d), wup_bf, cw, cb.reshape(1, ff2), wdown_bf, prev1, prev2)
    return y, up.reshape(b, seq_len, ff2)[:, seq_len - (CONV_W - 1):]


def _feature_major(x):
    lead = x.shape[:-3]
    n, h, dh = x.shape[-3:]
    nd = len(lead)
    return jnp.transpose(x, (*range(nd), nd + 1, nd + 2, nd)).reshape(*lead, h * dh, n)


def _position_major(xt, n_heads):
    lead = xt.shape[:-2]
    w, n = xt.shape[-2:]
    nd = len(lead)
    x4 = xt.reshape(*lead, n_heads, w // n_heads, n)
    return jnp.transpose(x4, (*range(nd), nd + 2, nd, nd + 1))


def kernel(x_prompt, x_sample, cache_mem_k, cache_mem_v, cache_sb_k, cache_sb_v, state_ffn_conv, page_table, mem_prompt, norm_mix, norm_ffn, norm_mem, w_in_a, gm_v_norm, gm_ws, gm_bs, w_in_b, sb_bias, w_mem_kv, mem_q_norm, mem_k_norm, w_out, w_up, conv_w, conv_b, w_down):
    depth = norm_mix.shape[0]
    bp, tp, d = x_prompt.shape
    bs_, ts, _ = x_sample.shape
    n_p, n_s = bp * tp, bs_ * ts
    sb_tile = 256

    k_gain_col = jnp.tile(mem_k_norm, (1, MEM_HEADS)).reshape(depth, MEM_WIDTH, 1)
    mem_kt_p, mem_vt_p = _mem_kv(mem_prompt, norm_mem, jnp.swapaxes(w_mem_kv, 1, 2).astype(BF16), k_gain_col)
    q_gain_full = jnp.tile(mem_q_norm, (1, MEM_HEADS))
    mem_kt_s = _feature_major(cache_mem_k)
    mem_vt_s = _feature_major(cache_mem_v)

    w_out_bf = w_out.astype(BF16)
    w_up_bf = w_up.astype(BF16)
    w_down_bf = w_down.astype(BF16)

    xp = x_prompt.reshape(n_p, d)
    xs = x_sample.reshape(n_s, d)
    sb_k_p, sb_v_p, sb_k_s, sb_v_s, gm_v_s, conv_p, conv_s = [], [], [], [], [], [], []
    for i in range(depth):
        if i % 2 == 0:
            a = i // 2
            w_bf = w_in_a[a].astype(BF16)
            bs_t = gm_bs[a].T
            mix_p, mq_p = _gmlp_in(xp, norm_mix[i], w_bf, gm_v_norm[a], gm_ws[a], bs_t,
                                   seq_len=CHUNK, cm=CHUNK, tm=512, emit_v=False)
            mix_s, mq_s, v_new = _gmlp_in(xs, norm_mix[i], w_bf, gm_v_norm[a], gm_ws[a], bs_t,
                                          seq_len=ts, cm=n_s, tm=n_s, emit_v=True)
            gm_v_s.append(v_new.reshape(bs_, ts, MIX_WIDTH))
        else:
            b = i // 2
            w = w_in_b[b]
            wq_bf = jnp.concatenate([w[:, :MIX_WIDTH], w[:, 3 * MIX_WIDTH:]], axis=1).astype(BF16)
            wkvt_bf = w[:, MIX_WIDTH:3 * MIX_WIDTH].T.astype(BF16)
            q_p, kt_p, vt_p, ktb_p, vtb_p, mq_p = _sb_in_prompt(
                xp.reshape(bp, tp, d), norm_mix[i], wq_bf, wkvt_bf, tm=512, tk=sb_tile)
            mq_p = mq_p.reshape(n_p, MEM_WIDTH)
            q_s, k_s, v_s, mq_s = _sb_in_rows(xs, norm_mix[i], w.astype(BF16))
            mix_p = _sb_prompt(q_p, ktb_p, vtb_p, sb_bias[b], tq=sb_tile).reshape(n_p, MIX_WIDTH)
            mix_s = _sb_decode(q_s.reshape(bs_, ts, MIX_WIDTH), k_s.reshape(bs_, ts, MIX_WIDTH),
                               v_s.reshape(bs_, ts, MIX_WIDTH),
                               _feature_major(cache_sb_k[b]), _feature_major(cache_sb_v[b]),
                               page_table, sb_bias[b]).reshape(n_s, MIX_WIDTH)
            sb_k_p.append(_position_major(kt_p, SB_HEADS))
            sb_v_p.append(_position_major(vt_p, SB_HEADS))
            sb_k_s.append(k_s.reshape(bs_, ts, SB_HEADS, SB_HEAD_DIM))
            sb_v_s.append(v_s.reshape(bs_, ts, SB_HEADS, SB_HEAD_DIM))
        xp3 = _mem_out(mix_p.reshape(bp, tp, MIX_WIDTH), mq_p.reshape(bp, tp, MEM_WIDTH), mem_kt_p, mem_vt_p,
                       q_gain_full[i], xp.reshape(bp, tp, d), w_out_bf, layer=i, tm=512)
        mem_out_s = _mem_attend(mq_s.reshape(bs_, ts, MEM_WIDTH), mem_kt_s, mem_vt_s, q_gain_full[i],
                                layer=i, tm=ts)
        xs = _out_proj(mix_s, mem_out_s.reshape(n_s, MEM_WIDTH), xs, w_out_bf, layer=i, tm=n_s)
        zero_rows = jnp.zeros((bp, CONV_W - 1, 2 * D_FF), F32)
        xp3, cp = _conv_ffn(xp3, norm_ffn[i], w_up_bf, conv_w[i], conv_b[i], w_down_bf,
                            zero_rows, layer=i, tm=512)
        xs, cs = _conv_ffn_rows(xs, norm_ffn[i], w_up_bf, conv_w[i], conv_b[i], w_down_bf,
                                state_ffn_conv[i], layer=i, seq_len=ts)
        xp = xp3.reshape(n_p, d)
        conv_p.append(cp)
        conv_s.append(cs)

    return (xp.reshape(bp, tp, d), xs.reshape(bs_, ts, d),
            _position_major(mem_kt_p, MEM_HEADS), _position_major(mem_vt_p, MEM_HEADS),
            jnp.stack(sb_k_p), jnp.stack(sb_v_p), jnp.stack(sb_k_s), jnp.stack(sb_v_s),
            jnp.stack(gm_v_s), jnp.stack(conv_p), jnp.stack(conv_s))
```

```python
import functools

import jax
import jax.numpy as jnp
from jax import lax
from jax.experimental import pallas as pl
from jax.experimental.pallas import tpu as pltpu

D_MODEL = 1024
MIX_WIDTH = 768
MEM_WIDTH = 256
CHUNK = 128
GM_GROUP_DIM = 128
GM_GROUPS = 6
SB_HEAD_DIM = 64
SB_HEADS = 12
MEM_TOKENS = 256
MEM_HEADS = 4
MEM_HEAD_DIM = 64
D_FF = 2816
CONV_W = 3
PAGE_SIZE = 128
EPS = 1e-6

VMEM_LIMIT = 56 * 1024 * 1024
F32 = jnp.float32
BF16 = jnp.bfloat16
LOG2E = 1.4426950408889634
SB_Q_SCALE = SB_HEAD_DIM ** -0.5 * LOG2E


def _cparams(sem):
    return pltpu.CompilerParams(dimension_semantics=sem, vmem_limit_bytes=VMEM_LIMIT)


def _rms(x, gain):
    ms = jnp.mean(x * x, axis=-1, keepdims=True)
    return x * lax.rsqrt(ms + EPS) * gain


def _nt_dot(a, b):
    return lax.dot_general(a, b, (((1,), (1,)), ((), ())), preferred_element_type=F32)


def _dot(a, b):
    return jnp.dot(a, b, preferred_element_type=F32)


def _split(x):
    hi = x.astype(BF16)
    return hi, (x - hi.astype(F32)).astype(BF16)


def _same_head(width, head_dim):
    r = lax.broadcasted_iota(jnp.int32, (width, width), 0) // head_dim
    c = lax.broadcasted_iota(jnp.int32, (width, width), 1) // head_dim
    return (r == c).astype(BF16)


def _head_stack(q, n_heads, head_dim):
    q = q.astype(F32)
    head = lax.broadcasted_iota(jnp.int32, q.shape, 1) // head_dim
    return jnp.concatenate([jnp.where(head == h, q, 0.0) for h in range(n_heads)], axis=0).astype(BF16)


def _head_unstack(o, n_heads, head_dim):
    t = o.shape[0] // n_heads
    head = lax.broadcasted_iota(jnp.int32, (t, o.shape[1]), 1) // head_dim
    out = jnp.zeros((t, o.shape[1]), o.dtype)
    for h in range(n_heads):
        out = out + jnp.where(head == h, o[h * t:(h + 1) * t], 0.0)
    return out


def _mem_kv_kernel(mem_ref, gain_ref, wt_ref, kgain_ref, k_ref, v_ref):
    h = _rms(mem_ref[0], gain_ref[0]).astype(BF16)
    mt = _nt_dot(wt_ref[0], h)
    kt = mt[:MEM_WIDTH]
    hi, lo = _split(kt * kt)
    same = _same_head(MEM_WIDTH, MEM_HEAD_DIM)
    ss = _dot(same, hi) + _dot(same, lo)
    k_ref[0, 0] = kt * lax.rsqrt(ss * (1.0 / MEM_HEAD_DIM) + EPS) * kgain_ref[0]
    v_ref[0, 0] = mt[MEM_WIDTH:]


def _mem_kv(mem, norm_mem, w_kv_t_bf, k_gain_col):
    depth = norm_mem.shape[0]
    batch, m_tok, d = mem.shape
    out = jax.ShapeDtypeStruct((depth, batch, MEM_WIDTH, m_tok), F32)
    return pl.pallas_call(
        _mem_kv_kernel,
        grid=(depth, batch),
        in_specs=[
            pl.BlockSpec((1, m_tok, d), lambda i, b: (b, 0, 0)),
            pl.BlockSpec((1, 1, d), lambda i, b: (i, 0, 0)),
            pl.BlockSpec((1, 2 * MEM_WIDTH, d), lambda i, b: (i, 0, 0)),
            pl.BlockSpec((1, MEM_WIDTH, 1), lambda i, b: (i, 0, 0)),
        ],
        out_specs=[
            pl.BlockSpec((1, 1, MEM_WIDTH, m_tok), lambda i, b: (i, b, 0, 0)),
            pl.BlockSpec((1, 1, MEM_WIDTH, m_tok), lambda i, b: (i, b, 0, 0)),
        ],
        out_shape=[out, out],
        compiler_params=_cparams(("arbitrary", "arbitrary")),
        name="mem_kv",
    )(mem, norm_mem.reshape(depth, 1, d), w_kv_t_bf, k_gain_col)


def _gmlp_in_kernel(x_ref, gain_ref, w_ref, vgain_ref, ws_ref, bst_ref, mix_ref, mq_ref, v_ref, *, seq_len, cm):
    tm = x_ref.shape[0]
    n_chains = 2 if tm >= 2 * cm else 1
    halves = [slice(c * (tm // n_chains), (c + 1) * (tm // n_chains)) for c in range(n_chains)]
    projs = [_dot(_rms(x_ref[rs, :], gain_ref[...]).astype(BF16), w_ref[...]) for rs in halves]
    for rs, proj in zip(halves, projs):
        mq_ref[rs, :] = proj[:, 2 * MIX_WIDTH:]
        v_ref[rs, :] = _rms(jax.nn.gelu(proj[:, MIX_WIDTH:2 * MIX_WIDTH]), vgain_ref[...])
        mix_ref[rs, :] = jax.nn.gelu(proj[:, :MIX_WIDTH])
    r = lax.broadcasted_iota(jnp.int32, (cm, cm), 0)
    c = lax.broadcasted_iota(jnp.int32, (cm, cm), 1)
    allowed = (r // seq_len == c // seq_len) & (c % seq_len <= r % seq_len)
    if seq_len != cm:
        pick = (lax.broadcasted_iota(jnp.int32, (cm, CHUNK), 1)
                == lax.broadcasted_iota(jnp.int32, (cm, CHUNK), 0) % seq_len).astype(BF16)
    for g in range(GM_GROUPS):
        cols = slice(g * GM_GROUP_DIM, (g + 1) * GM_GROUP_DIM)
        wg = ws_ref[g].astype(BF16)
        if seq_len != cm:
            wg = _nt_dot(_dot(pick, wg).astype(BF16), pick)
        wg = jnp.where(allowed, wg, 0.0).astype(BF16)
        bias_col = jnp.concatenate([bst_ref[:seq_len, g:g + 1]] * (cm // seq_len), axis=0)
        for ci in range(tm // cm):
            rows = slice(ci * cm, (ci + 1) * cm)
            mixed = _dot(wg, v_ref[rows, cols].astype(BF16)) + bias_col
            mix_ref[rows, cols] = mix_ref[rows, cols] * mixed


def _gmlp_in(x, gain, w_bf, v_gain, ws, bs_t, *, seq_len, cm, tm, emit_v):
    n, d = x.shape
    d_in = w_bf.shape[1]
    const = lambda i: (0, 0)
    out_shape = [jax.ShapeDtypeStruct((n, MIX_WIDTH), F32), jax.ShapeDtypeStruct((n, MEM_WIDTH), F32)]
    out_specs = [pl.BlockSpec((tm, MIX_WIDTH), lambda i: (i, 0)), pl.BlockSpec((tm, MEM_WIDTH), lambda i: (i, 0))]
    scratch = []
    if emit_v:
        out_shape.append(jax.ShapeDtypeStruct((n, MIX_WIDTH), F32))
        out_specs.append(pl.BlockSpec((tm, MIX_WIDTH), lambda i: (i, 0)))
    else:
        scratch.append(pltpu.VMEM((tm, MIX_WIDTH), F32))
    return pl.pallas_call(
        functools.partial(_gmlp_in_kernel, seq_len=seq_len, cm=cm),
        grid=(n // tm,),
        in_specs=[
            pl.BlockSpec((tm, d), lambda i: (i, 0)),
            pl.BlockSpec((1, d), const),
            pl.BlockSpec((d, d_in), const),
            pl.BlockSpec((1, MIX_WIDTH), const),
            pl.BlockSpec((GM_GROUPS, CHUNK, CHUNK), lambda i: (0, 0, 0)),
            pl.BlockSpec((CHUNK, GM_GROUPS), const),
        ],
        out_specs=out_specs,
        out_shape=out_shape,
        scratch_shapes=scratch,
        compiler_params=_cparams(("arbitrary",)),
        name="gmlp_in",
    )(x, gain.reshape(1, d), w_bf, v_gain.reshape(1, MIX_WIDTH), ws, bs_t)


def _sb_in_rows_kernel(x_ref, gain_ref, w_ref, q_ref, k_ref, v_ref, mq_ref):
    h = _rms(x_ref[...], gain_ref[...]).astype(BF16)
    proj = _dot(h, w_ref[...])
    q_ref[...] = (proj[:, :MIX_WIDTH] * SB_Q_SCALE).astype(BF16)
    k_ref[...] = proj[:, MIX_WIDTH:2 * MIX_WIDTH]
    v_ref[...] = proj[:, 2 * MIX_WIDTH:3 * MIX_WIDTH]
    mq_ref[...] = proj[:, 3 * MIX_WIDTH:]


def _sb_in_rows(x, gain, w_bf):
    n, d = x.shape
    full = lambda width: pl.BlockSpec((n, width), lambda i: (0, 0))
    shp = lambda width, dt: jax.ShapeDtypeStruct((n, width), dt)
    return pl.pallas_call(
        _sb_in_rows_kernel,
        grid=(1,),
        in_specs=[full(d), pl.BlockSpec((1, d), lambda i: (0, 0)), pl.BlockSpec(w_bf.shape, lambda i: (0, 0))],
        out_specs=[full(MIX_WIDTH)] * 3 + [full(MEM_WIDTH)],
        out_shape=[shp(MIX_WIDTH, BF16), shp(MIX_WIDTH, F32), shp(MIX_WIDTH, F32), shp(MEM_WIDTH, F32)],
        compiler_params=_cparams(("arbitrary",)),
        name="sb_in_rows",
    )(x, gain.reshape(1, d), w_bf)


def _sb_in_prompt_kernel(x_ref, gain_ref, wq_ref, wkvt_ref, q_ref, kt_ref, vt_ref, ktb_ref, vtb_ref, mq_ref, *, tk):
    tm = x_ref.shape[1]
    h = _rms(x_ref[0], gain_ref[...]).astype(BF16)
    qm = _dot(h, wq_ref[...])
    q_ref[0] = (qm[:, :MIX_WIDTH] * SB_Q_SCALE).astype(BF16)
    mq_ref[0] = qm[:, MIX_WIDTH:]
    kvt = _nt_dot(wkvt_ref[...], h)
    kt_ref[0] = kvt[:MIX_WIDTH]
    vt_ref[0] = kvt[MIX_WIDTH:]
    for c in range(tm // tk):
        ktb_ref[0, c] = kvt[:MIX_WIDTH, c * tk:(c + 1) * tk].astype(BF16)
        vtb_ref[0, c] = kvt[MIX_WIDTH:, c * tk:(c + 1) * tk].astype(BF16)


def _sb_in_prompt(x, gain, wq_bf, wkvt_bf, *, tm, tk):
    b, t, d = x.shape
    const = lambda bi, i: (0, 0)
    return pl.pallas_call(
        functools.partial(_sb_in_prompt_kernel, tk=tk),
        grid=(b, t // tm),
        in_specs=[
            pl.BlockSpec((1, tm, d), lambda bi, i: (bi, i, 0)),
            pl.BlockSpec((1, d), const),
            pl.BlockSpec((d, MIX_WIDTH + MEM_WIDTH), const),
            pl.BlockSpec((2 * MIX_WIDTH, d), const),
        ],
        out_specs=[
            pl.BlockSpec((1, tm, MIX_WIDTH), lambda bi, i: (bi, i, 0)),
            pl.BlockSpec((1, MIX_WIDTH, tm), lambda bi, i: (bi, 0, i)),
            pl.BlockSpec((1, MIX_WIDTH, tm), lambda bi, i: (bi, 0, i)),
            pl.BlockSpec((1, tm // tk, MIX_WIDTH, tk), lambda bi, i: (bi, i, 0, 0)),
            pl.BlockSpec((1, tm // tk, MIX_WIDTH, tk), lambda bi, i: (bi, i, 0, 0)),
            pl.BlockSpec((1, tm, MEM_WIDTH), lambda bi, i: (bi, i, 0)),
        ],
        out_shape=[
            jax.ShapeDtypeStruct((b, t, MIX_WIDTH), BF16),
            jax.ShapeDtypeStruct((b, MIX_WIDTH, t), F32),
            jax.ShapeDtypeStruct((b, MIX_WIDTH, t), F32),
            jax.ShapeDtypeStruct((b, t // tk, MIX_WIDTH, tk), BF16),
            jax.ShapeDtypeStruct((b, t // tk, MIX_WIDTH, tk), BF16),
            jax.ShapeDtypeStruct((b, t, MEM_WIDTH), F32),
        ],
        compiler_params=_cparams(("arbitrary", "arbitrary")),
        name="sb_in_prompt",
    )(x, gain.reshape(1, d), wq_bf, wkvt_bf)


def _sb_block(q_stack, k, v, bias_col, neg_tri, carry, mask, feature_major, n_chains=1):
    tk = neg_tri.shape[0]
    n_rows = q_stack.shape[0]
    chains = [slice(c * (n_rows // n_chains), (c + 1) * (n_rows // n_chains)) for c in range(n_chains)]
    masks = [None if mask is None else mask[r] for r in chains]
    ws = [(_dot(q_stack[r], k) if feature_major else _nt_dot(q_stack[r], k)) + bias_col[r] for r in chains]
    sums = []
    for w, m in zip(ws, masks):
        sp = jnp.maximum(w, 0.0) + jnp.log(1.0 + jnp.exp2(-jnp.abs(w))) * LOG2E
        if m is not None:
            sp = jnp.where(m, sp, 0.0)
        sp_bf = sp.astype(BF16)
        sums.append([_dot(sp_bf[:, j * tk:(j + 1) * tk], neg_tri) for j in reversed(range(sp.shape[1] // tk))])
    contribs, carries = [], []
    for w, m, r, from_here in zip(ws, masks, chains, sums):
        c = carry[r]
        parts = []
        for i, fh in enumerate(from_here):
            j = len(from_here) - 1 - i
            parts.append(w[:, j * tk:(j + 1) * tk] + (fh + c))
            c = c + fh[:, 0:1]
        a = jnp.exp2(parts[0] if len(parts) == 1 else jnp.concatenate(parts[::-1], axis=1))
        if m is not None:
            a = jnp.where(m, a, 0.0)
        a = a.astype(BF16)
        contribs.append(_nt_dot(a, v) if feature_major else _dot(a, v))
        carries.append(c)
    if n_chains == 1:
        return contribs[0], carries[0]
    return jnp.concatenate(contribs, axis=0), jnp.concatenate(carries, axis=0)


def _neg_tri(tk):
    j = lax.broadcasted_iota(jnp.int32, (tk, tk), 0)
    s = lax.broadcasted_iota(jnp.int32, (tk, tk), 1)
    return jnp.where(j >= s, -1.0, 0.0).astype(BF16)


HEADS_PER_GROUP = 4
GROUP_WIDTH = HEADS_PER_GROUP * SB_HEAD_DIM
SB_RUN = 4
SB_CHAINS = 2


def _sb_prompt_kernel(bias_ref, q_ref, k_ref, v_ref, o_ref, qs_ref, acc_ref, carry_ref, *, tq):
    hg = pl.program_id(1)
    qi = pl.program_id(2)
    rows = HEADS_PER_GROUP * tq
    qs_ref[...] = _head_stack(q_ref[0], HEADS_PER_GROUP, SB_HEAD_DIM)
    bias_col = jnp.concatenate(
        [jnp.full((tq, 1), bias_ref[hg * HEADS_PER_GROUP + h] * LOG2E, F32) for h in range(HEADS_PER_GROUP)], axis=0)
    tri = _neg_tri(tq)

    def key_run(kj, n_blocks, diagonal):
        k = jnp.concatenate([k_ref[0, kj + i] for i in range(n_blocks)], axis=1)
        v = jnp.concatenate([v_ref[0, kj + i] for i in range(n_blocks)], axis=1)
        if diagonal:
            t_row = lax.broadcasted_iota(jnp.int32, (rows, tq), 0) % tq
            mask = lax.broadcasted_iota(jnp.int32, (rows, tq), 1) < t_row
            carry = jnp.zeros((rows, 1), F32)
        else:
            mask = None
            carry = carry_ref[...]
        contrib, carry = _sb_block(qs_ref[...], k, v, bias_col, tri, carry, mask, True, n_chains=SB_CHAINS)
        if diagonal:
            acc_ref[...] = contrib
        else:
            acc_ref[...] += contrib
        carry_ref[...] = carry

    key_run(qi, 1, True)
    n_runs = qi // SB_RUN

    def single(i, _):
        key_run(qi - 1 - i, 1, False)
        return 0

    lax.fori_loop(0, qi - n_runs * SB_RUN, single, 0)

    def run(i, _):
        key_run(SB_RUN * (n_runs - 1 - i), SB_RUN, False)
        return 0

    lax.fori_loop(0, n_runs, run, 0)
    o_ref[0] = _head_unstack(acc_ref[...], HEADS_PER_GROUP, SB_HEAD_DIM)


def _sb_prompt(q_bf, kt_bf, vt_bf, bias, *, tq):
    b, t, w = q_bf.shape
    nkb = t // tq
    return pl.pallas_call(
        functools.partial(_sb_prompt_kernel, tq=tq),
        grid=(b, w // GROUP_WIDTH, nkb),
        in_specs=[
            pl.BlockSpec(memory_space=pltpu.SMEM),
            pl.BlockSpec((1, tq, GROUP_WIDTH), lambda bi, g, qi: (bi, qi, g)),
            pl.BlockSpec((1, nkb, GROUP_WIDTH, tq), lambda bi, g, qi: (bi, 0, g, 0)),
            pl.BlockSpec((1, nkb, GROUP_WIDTH, tq), lambda bi, g, qi: (bi, 0, g, 0)),
        ],
        out_specs=pl.BlockSpec((1, tq, GROUP_WIDTH), lambda bi, g, qi: (bi, qi, g)),
        out_shape=jax.ShapeDtypeStruct((b, t, w), F32),
        scratch_shapes=[pltpu.VMEM((HEADS_PER_GROUP * tq, GROUP_WIDTH), BF16),
                        pltpu.VMEM((HEADS_PER_GROUP * tq, GROUP_WIDTH), F32),
                        pltpu.VMEM((HEADS_PER_GROUP * tq, 1), F32)],
        compiler_params=_cparams(("arbitrary", "arbitrary", "arbitrary")),
        name="sb_prompt",
    )(bias, q_bf, kt_bf, vt_bf)


PAGES_PER_STEP = 16


def _sb_decode_kernel(pt_ref, bias_ref, q_ref, kn_ref, vn_ref, *refs, t_new):
    k_pages = refs[:PAGES_PER_STEP]
    v_pages = refs[PAGES_PER_STEP:2 * PAGES_PER_STEP]
    o_ref, acc_ref, carry_ref = refs[2 * PAGES_PER_STEP:]
    j = pl.program_id(1)
    rows = SB_HEADS * t_new
    q_stack = _head_stack(q_ref[0], SB_HEADS, SB_HEAD_DIM)
    bias_col = jnp.concatenate([jnp.full((t_new, 1), bias_ref[h] * LOG2E, F32) for h in range(SB_HEADS)], axis=0)

    @pl.when(j == 0)
    def _():
        pad = jnp.zeros((PAGE_SIZE - t_new, MIX_WIDTH), F32)
        k = jnp.concatenate([kn_ref[0], pad], axis=0).astype(BF16)
        v = jnp.concatenate([vn_ref[0], pad], axis=0).astype(BF16)
        t_row = lax.broadcasted_iota(jnp.int32, (rows, PAGE_SIZE), 0) % t_new
        s_col = lax.broadcasted_iota(jnp.int32, (rows, PAGE_SIZE), 1)
        contrib, carry = _sb_block(q_stack, k, v, bias_col, _neg_tri(PAGE_SIZE),
                                   jnp.zeros((rows, 1), F32), s_col < t_row, False)
        acc_ref[...] = contrib
        carry_ref[...] = carry

    k = jnp.concatenate([r[0].astype(BF16) for r in k_pages], axis=1)
    v = jnp.concatenate([r[0].astype(BF16) for r in v_pages], axis=1)
    contrib, carry = _sb_block(q_stack, k, v, bias_col, _neg_tri(2 * PAGE_SIZE), carry_ref[...], None, True)
    acc_ref[...] += contrib
    carry_ref[...] = carry

    @pl.when(j == pl.num_programs(1) - 1)
    def _():
        o_ref[0] = _head_unstack(acc_ref[...], SB_HEADS, SB_HEAD_DIM)


def _sb_decode(q_bf, k_new, v_new, cache_kt, cache_vt, page_table, bias):
    b, t_new, w = q_bf.shape
    n_pages = page_table.shape[1]
    n_steps = n_pages // PAGES_PER_STEP

    def page_spec(slot):
        def index_map(bi, j, pt, bias):
            return (pt[bi, (n_steps - 1 - j) * PAGES_PER_STEP + slot], 0, 0)
        return pl.BlockSpec((1, w, PAGE_SIZE), index_map)

    new_spec = pl.BlockSpec((1, t_new, w), lambda bi, j, pt, bias: (bi, 0, 0))
    grid_spec = pltpu.PrefetchScalarGridSpec(
        num_scalar_prefetch=2,
        grid=(b, n_steps),
        in_specs=[new_spec, new_spec, new_spec]
        + [page_spec(s) for s in range(PAGES_PER_STEP)] * 2,
        out_specs=new_spec,
        scratch_shapes=[pltpu.VMEM((SB_HEADS * t_new, w), F32), pltpu.VMEM((SB_HEADS * t_new, 1), F32)],
    )
    return pl.pallas_call(
        functools.partial(_sb_decode_kernel, t_new=t_new),
        grid_spec=grid_spec,
        out_shape=jax.ShapeDtypeStruct((b, t_new, w), F32),
        compiler_params=_cparams(("arbitrary", "arbitrary")),
        name="sb_decode",
    )(page_table, bias, q_bf, k_new, v_new, *([cache_kt] * PAGES_PER_STEP), *([cache_vt] * PAGES_PER_STEP))


def _mem_attention_chains(qs, kt, vt, gain):
    same = _same_head(MEM_WIDTH, MEM_HEAD_DIM)
    kt = kt.astype(BF16)
    vt = vt.astype(BF16)
    sq = []
    for q in qs:
        hi, lo = _split(q * q)
        sq.append(_dot(hi, same) + _dot(lo, same))
    scores = []
    for q, ss in zip(qs, sq):
        qn = q * lax.rsqrt(ss * (1.0 / MEM_HEAD_DIM) + EPS) * gain
        q_stack = _head_stack(qn * (MEM_HEAD_DIM ** -0.5), MEM_HEADS, MEM_HEAD_DIM)
        scores.append(_dot(q_stack, kt))
    outs = []
    for s in scores:
        p = jnp.exp(s - jnp.max(s, axis=-1, keepdims=True))
        l = jnp.sum(p, axis=-1, keepdims=True)
        outs.append((_nt_dot(p.astype(BF16), vt), l))
    return [_head_unstack(o / l, MEM_HEADS, MEM_HEAD_DIM) for o, l in outs]


def _mem_attention(q, kt, vt, gain):
    return _mem_attention_chains([q], kt, vt, gain)[0]


def _mem_attend_kernel(q_ref, kt_ref, vt_ref, gain_ref, o_ref):
    for s in range(q_ref.shape[0]):
        o_ref[s] = _mem_attention(q_ref[s], kt_ref[s], vt_ref[s], gain_ref[...])


def _mem_attend(q, mkt, mvt, q_gain_full, *, layer, seqs):
    b, t, w = q.shape
    kv_spec = pl.BlockSpec((None, seqs, w, MEM_TOKENS), lambda bi: (layer, bi, 0, 0))
    return pl.pallas_call(
        _mem_attend_kernel,
        grid=(b // seqs,),
        in_specs=[
            pl.BlockSpec((seqs, t, w), lambda bi: (bi, 0, 0)),
            kv_spec,
            kv_spec,
            pl.BlockSpec((1, w), lambda bi: (0, 0)),
        ],
        out_specs=pl.BlockSpec((seqs, t, w), lambda bi: (bi, 0, 0)),
        out_shape=jax.ShapeDtypeStruct((b, t, w), F32),
        compiler_params=_cparams(("arbitrary",)),
        name="mem_attend",
    )(q, mkt, mvt, q_gain_full.reshape(1, w))


def _out_proj_kernel(mix_ref, mem_ref, x_ref, w_ref, y_ref):
    y = _dot(mix_ref[...].astype(BF16), w_ref[:MIX_WIDTH, :])
    y = y + _dot(mem_ref[...].astype(BF16), w_ref[MIX_WIDTH:, :])
    y_ref[...] = x_ref[...] + y


def _out_proj(mix, mem_out, x, w_bf, *, layer, tm):
    n, d = x.shape
    return pl.pallas_call(
        _out_proj_kernel,
        grid=(n // tm,),
        in_specs=[
            pl.BlockSpec((tm, MIX_WIDTH), lambda i: (i, 0)),
            pl.BlockSpec((tm, MEM_WIDTH), lambda i: (i, 0)),
            pl.BlockSpec((tm, d), lambda i: (i, 0)),
            pl.BlockSpec((None, d, d), lambda i: (layer, 0, 0)),
        ],
        out_specs=pl.BlockSpec((tm, d), lambda i: (i, 0)),
        out_shape=jax.ShapeDtypeStruct((n, d), F32),
        compiler_params=_cparams(("arbitrary",)),
        name="out_proj",
    )(mix, mem_out, x, w_bf)


MEM_CHAINS = 2


def _mem_out_kernel(mix_ref, mq_ref, kt_ref, vt_ref, gain_ref, x_ref, w_ref, y_ref):
    tm = x_ref.shape[1]
    chunks = [slice(c * (tm // MEM_CHAINS), (c + 1) * (tm // MEM_CHAINS)) for c in range(MEM_CHAINS)]
    ys = [_dot(mix_ref[0, r].astype(BF16), w_ref[:MIX_WIDTH, :]) for r in chunks]
    mems = _mem_attention_chains([mq_ref[0, r] for r in chunks], kt_ref[...], vt_ref[...], gain_ref[...])
    for r, y, mem in zip(chunks, ys, mems):
        y_ref[0, r] = x_ref[0, r] + (y + _dot(mem.astype(BF16), w_ref[MIX_WIDTH:, :]))


def _mem_out(mix, mq, mkt, mvt, q_gain_full, x, w_bf, *, layer, tm):
    b, t, d = x.shape
    rows = lambda width: pl.BlockSpec((1, tm, width), lambda bi, i: (bi, i, 0))
    kv_spec = pl.BlockSpec((None, None, MEM_WIDTH, MEM_TOKENS), lambda bi, i: (layer, bi, 0, 0))
    return pl.pallas_call(
        _mem_out_kernel,
        grid=(b, t // tm),
        in_specs=[rows(MIX_WIDTH), rows(MEM_WIDTH), kv_spec, kv_spec,
                  pl.BlockSpec((1, MEM_WIDTH), lambda bi, i: (0, 0)), rows(d),
                  pl.BlockSpec((None, d, d), lambda bi, i: (layer, 0, 0))],
        out_specs=rows(d),
        out_shape=jax.ShapeDtypeStruct((b, t, d), F32),
        compiler_params=_cparams(("arbitrary", "arbitrary")),
        name="mem_out",
    )(mix, mq, mkt, mvt, q_gain_full.reshape(1, MEM_WIDTH), x, w_bf)


FF_CHUNK = 256
EXT_PAD = 8
UP_AHEAD = 11


def _conv_ffn_kernel(x_ref, gain_ref, wup_ref, cw_ref, cb_ref, wdown_ref, past_ref,
                     y_ref, state_ref, ext_ref, h_ref, acc_ref):
    t = pl.program_id(1)
    tm = x_ref.shape[1]
    lo = EXT_PAD - (CONV_W - 1)

    @pl.when(t == 0)
    def _():
        ext_ref[lo:EXT_PAD, :] = past_ref[0]

    @pl.when(t > 0)
    def _():
        ext_ref[lo:EXT_PAD, :] = ext_ref[tm + lo:tm + EXT_PAD, :]

    x = x_ref[0]
    h_ref[...] = _rms(x, gain_ref[...]).astype(BF16)
    acc_ref[...] = x
    n_chunks = D_FF // FF_CHUNK

    def chunk_cols(c):
        return [slice(half * D_FF + c * FF_CHUNK, half * D_FF + (c + 1) * FF_CHUNK) for half in range(2)]

    def up_project(c):
        for cols in chunk_cols(c):
            ext_ref[EXT_PAD:EXT_PAD + tm, cols] = _dot(h_ref[...], wup_ref[:, cols])

    for c in range(min(UP_AHEAD, n_chunks)):
        up_project(c)
    for c in range(n_chunks):
        if c + UP_AHEAD < n_chunks:
            up_project(c + UP_AHEAD)
        conv = []
        for cols in chunk_cols(c):
            y = cb_ref[:, cols]
            for j in range(CONV_W):
                y = y + cw_ref[j:j + 1, cols] * ext_ref[lo + j:lo + j + tm, cols]
            conv.append(y)
        g = (jax.nn.silu(conv[0]) * conv[1]).astype(BF16)
        acc_ref[...] += _dot(g, wdown_ref[c * FF_CHUNK:(c + 1) * FF_CHUNK, :])
    y_ref[0] = acc_ref[...]
    state_ref[0] = ext_ref[tm + lo:tm + EXT_PAD, :]


def _conv_ffn(x, gain, wup_bf, cw, cb, wdown_bf, past, *, layer, tm):
    b, t, d = x.shape
    ff2 = 2 * D_FF
    const = lambda bi, i: (0, 0)
    return pl.pallas_call(
        _conv_ffn_kernel,
        grid=(b, t // tm),
        in_specs=[
            pl.BlockSpec((1, tm, d), lambda bi, i: (bi, i, 0)),
            pl.BlockSpec((1, d), const),
            pl.BlockSpec((None, d, ff2), lambda bi, i: (layer, 0, 0), pipeline_mode=pl.Buffered(1)),
            pl.BlockSpec((CONV_W, ff2), const),
            pl.BlockSpec((1, ff2), const),
            pl.BlockSpec((None, D_FF, d), lambda bi, i: (layer, 0, 0), pipeline_mode=pl.Buffered(1)),
            pl.BlockSpec((1, CONV_W - 1, ff2), lambda bi, i: (bi, 0, 0)),
        ],
        out_specs=[
            pl.BlockSpec((1, tm, d), lambda bi, i: (bi, i, 0)),
            pl.BlockSpec((1, CONV_W - 1, ff2), lambda bi, i: (bi, 0, 0)),
        ],
        out_shape=[jax.ShapeDtypeStruct((b, t, d), F32), jax.ShapeDtypeStruct((b, CONV_W - 1, ff2), F32)],
        scratch_shapes=[pltpu.VMEM((EXT_PAD + tm, ff2), F32), pltpu.VMEM((tm, d), BF16), pltpu.VMEM((tm, d), F32)],
        compiler_params=_cparams(("arbitrary", "arbitrary")),
        name="conv_ffn",
    )(x, gain.reshape(1, d), wup_bf, cw, cb.reshape(1, ff2), wdown_bf, past)


def _conv_ffn_rows_kernel(x_ref, gain_ref, wup_ref, cw_ref, cb_ref, wdown_ref, prev1_ref, prev2_ref,
                          y_ref, up_ref, h_ref, acc_ref, *, seq_len):
    n = x_ref.shape[0]
    x = x_ref[...]
    h_ref[...] = _rms(x, gain_ref[...]).astype(BF16)
    acc_ref[...] = x
    t_idx = lax.broadcasted_iota(jnp.int32, (n, FF_CHUNK), 0) % seq_len
    n_chunks = D_FF // FF_CHUNK
    for c in range(n_chunks):
        for half in range(2):
            cols = slice(half * D_FF + c * FF_CHUNK, half * D_FF + (c + 1) * FF_CHUNK)
            up_ref[:, cols] = _dot(h_ref[...], wup_ref[:, cols])
    for c in range(n_chunks):
        conv = []
        for half in range(2):
            cols = slice(half * D_FF + c * FF_CHUNK, half * D_FF + (c + 1) * FF_CHUNK)
            up = up_ref[:, cols]
            back1 = jnp.where(t_idx >= 1, pltpu.roll(up, 1, 0), prev1_ref[:, cols])
            back2 = jnp.where(t_idx >= 2, pltpu.roll(up, 2, 0), prev2_ref[:, cols])
            conv.append(cb_ref[:, cols] + cw_ref[2:3, cols] * up + cw_ref[1:2, cols] * back1
                        + cw_ref[0:1, cols] * back2)
        g = (jax.nn.silu(conv[0]) * conv[1]).astype(BF16)
        acc_ref[...] += _dot(g, wdown_ref[c * FF_CHUNK:(c + 1) * FF_CHUNK, :])
    y_ref[...] = acc_ref[...]


def _conv_ffn_rows(x, gain, wup_bf, cw, cb, wdown_bf, past, *, layer, seq_len):
    n, d = x.shape
    ff2 = 2 * D_FF
    b = n // seq_len
    prev2 = jnp.pad(past, ((0, 0), (0, seq_len - 2), (0, 0))).reshape(n, ff2)
    prev1 = jnp.pad(past[:, 1:], ((0, 0), (0, seq_len - 1), (0, 0))).reshape(n, ff2)
    tm = 128
    const = lambda i: (0, 0)
    rows = lambda width: pl.BlockSpec((tm, width), lambda i: (i, 0))
    y, up = pl.pallas_call(
        functools.partial(_conv_ffn_rows_kernel, seq_len=seq_len),
        grid=(n // tm,),
        in_specs=[rows(d), pl.BlockSpec((1, d), const),
                  pl.BlockSpec((None, d, ff2), lambda i: (layer, 0, 0), pipeline_mode=pl.Buffered(1)),
                  pl.BlockSpec((CONV_W, ff2), const), pl.BlockSpec((1, ff2), const),
                  pl.BlockSpec((None, D_FF, d), lambda i: (layer, 0, 0), pipeline_mode=pl.Buffered(1)),
                  rows(ff2), rows(ff2)],
        out_specs=[rows(d), rows(ff2)],
        out_shape=[jax.ShapeDtypeStruct((n, d), F32), jax.ShapeDtypeStruct((n, ff2), F32)],
        scratch_shapes=[pltpu.VMEM((tm, d), BF16), pltpu.VMEM((tm, d), F32)],
        compiler_params=_cparams(("arbitrary",)),
        name="conv_ffn_rows",
    )(x, gain.reshape(1, d), wup_bf, cw, cb.reshape(1, ff2), wdown_bf, prev1, prev2)
    return y, up.reshape(b, seq_len, ff2)[:, seq_len - (CONV_W - 1):]


def _feature_major(x):
    lead = x.shape[:-3]
    n, h, dh = x.shape[-3:]
    nd = len(lead)
    return jnp.transpose(x, (*range(nd), nd + 1, nd + 2, nd)).reshape(*lead, h * dh, n)


def _position_major(xt, n_heads):
    lead = xt.shape[:-2]
    w, n = xt.shape[-2:]
    nd = len(lead)
    x4 = xt.reshape(*lead, n_heads, w // n_heads, n)
    return jnp.transpose(x4, (*range(nd), nd + 2, nd, nd + 1))


def kernel(x_prompt, x_sample, cache_mem_k, cache_mem_v, cache_sb_k, cache_sb_v, state_ffn_conv, page_table, mem_prompt, norm_mix, norm_ffn, norm_mem, w_in_a, gm_v_norm, gm_ws, gm_bs, w_in_b, sb_bias, w_mem_kv, mem_q_norm, mem_k_norm, w_out, w_up, conv_w, conv_b, w_down):
    depth = norm_mix.shape[0]
    bp, tp, d = x_prompt.shape
    bs_, ts, _ = x_sample.shape
    n_p, n_s = bp * tp, bs_ * ts
    sb_tile = 256

    k_gain_col = jnp.tile(mem_k_norm, (1, MEM_HEADS)).reshape(depth, MEM_WIDTH, 1)
    mem_kt_p, mem_vt_p = _mem_kv(mem_prompt, norm_mem, jnp.swapaxes(w_mem_kv, 1, 2).astype(BF16), k_gain_col)
    q_gain_full = jnp.tile(mem_q_norm, (1, MEM_HEADS))
    mem_kt_s = _feature_major(cache_mem_k)
    mem_vt_s = _feature_major(cache_mem_v)

    w_out_bf = w_out.astype(BF16)
    w_up_bf = w_up.astype(BF16)
    w_down_bf = w_down.astype(BF16)

    xp = x_prompt.reshape(n_p, d)
    xs = x_sample.reshape(n_s, d)
    sb_k_p, sb_v_p, sb_k_s, sb_v_s, gm_v_s, conv_p, conv_s = [], [], [], [], [], [], []
    for i in range(depth):
        if i % 2 == 0:
            a = i // 2
            w_bf = w_in_a[a].astype(BF16)
            bs_t = gm_bs[a].T
            mix_p, mq_p = _gmlp_in(xp, norm_mix[i], w_bf, gm_v_norm[a], gm_ws[a], bs_t,
                                   seq_len=CHUNK, cm=CHUNK, tm=512, emit_v=False)
            mix_s, mq_s, v_new = _gmlp_in(xs, norm_mix[i], w_bf, gm_v_norm[a], gm_ws[a], bs_t,
                                          seq_len=ts, cm=n_s, tm=n_s, emit_v=True)
            gm_v_s.append(v_new.reshape(bs_, ts, MIX_WIDTH))
        else:
            b = i // 2
            w = w_in_b[b]
            wq_bf = jnp.concatenate([w[:, :MIX_WIDTH], w[:, 3 * MIX_WIDTH:]], axis=1).astype(BF16)
            wkvt_bf = w[:, MIX_WIDTH:3 * MIX_WIDTH].T.astype(BF16)
            q_p, kt_p, vt_p, ktb_p, vtb_p, mq_p = _sb_in_prompt(
                xp.reshape(bp, tp, d), norm_mix[i], wq_bf, wkvt_bf, tm=512, tk=sb_tile)
            mq_p = mq_p.reshape(n_p, MEM_WIDTH)
            q_s, k_s, v_s, mq_s = _sb_in_rows(xs, norm_mix[i], w.astype(BF16))
            mix_p = _sb_prompt(q_p, ktb_p, vtb_p, sb_bias[b], tq=sb_tile).reshape(n_p, MIX_WIDTH)
            mix_s = _sb_decode(q_s.reshape(bs_, ts, MIX_WIDTH), k_s.reshape(bs_, ts, MIX_WIDTH),
                               v_s.reshape(bs_, ts, MIX_WIDTH),
                               _feature_major(cache_sb_k[b]), _feature_major(cache_sb_v[b]),
                               page_table, sb_bias[b]).reshape(n_s, MIX_WIDTH)
            sb_k_p.append(_position_major(kt_p, SB_HEADS))
            sb_v_p.append(_position_major(vt_p, SB_HEADS))
            sb_k_s.append(k_s.reshape(bs_, ts, SB_HEADS, SB_HEAD_DIM))
            sb_v_s.append(v_s.reshape(bs_, ts, SB_HEADS, SB_HEAD_DIM))
        xp3 = _mem_out(mix_p.reshape(bp, tp, MIX_WIDTH), mq_p.reshape(bp, tp, MEM_WIDTH), mem_kt_p, mem_vt_p,
                       q_gain_full[i], xp.reshape(bp, tp, d), w_out_bf, layer=i, tm=1024)
        mem_out_s = _mem_attend(mq_s.reshape(bs_, ts, MEM_WIDTH), mem_kt_s, mem_vt_s, q_gain_full[i],
                                layer=i, seqs=8)
        xs = _out_proj(mix_s, mem_out_s.reshape(n_s, MEM_WIDTH), xs, w_out_bf, layer=i, tm=n_s)
        zero_rows = jnp.zeros((bp, CONV_W - 1, 2 * D_FF), F32)
        xp3, cp = _conv_ffn(xp3, norm_ffn[i], w_up_bf, conv_w[i], conv_b[i], w_down_bf,
                            zero_rows, layer=i, tm=512)
        xs, cs = _conv_ffn_rows(xs, norm_ffn[i], w_up_bf, conv_w[i], conv_b[i], w_down_bf,
                                state_ffn_conv[i], layer=i, seq_len=ts)
        xp = xp3.reshape(n_p, d)
        conv_p.append(cp)
        conv_s.append(cs)

    return (xp.reshape(bp, tp, d), xs.reshape(bs_, ts, d),
            _position_major(mem_kt_p, MEM_HEADS), _position_major(mem_vt_p, MEM_HEADS),
            jnp.stack(sb_k_p), jnp.stack(sb_v_p), jnp.stack(sb_k_s), jnp.stack(sb_v_s),
            jnp.stack(gm_v_s), jnp.stack(conv_p), jnp.stack(conv_s))
```

```python
import functools

import jax
import jax.numpy as jnp
from jax import lax
from jax.experimental import pallas as pl
from jax.experimental.pallas import tpu as pltpu

D_MODEL = 1024
MIX_WIDTH = 768
MEM_WIDTH = 256
CHUNK = 128
GM_GROUP_DIM = 128
GM_GROUPS = 6
SB_HEAD_DIM = 64
SB_HEADS = 12
MEM_TOKENS = 256
MEM_HEADS = 4
MEM_HEAD_DIM = 64
D_FF = 2816
CONV_W = 3
PAGE_SIZE = 128
EPS = 1e-6

VMEM_LIMIT = 56 * 1024 * 1024
F32 = jnp.float32
BF16 = jnp.bfloat16
LOG2E = 1.4426950408889634
SB_Q_SCALE = SB_HEAD_DIM ** -0.5 * LOG2E
SOFTPLUS_LINEAR_FROM = 64.0


def _cparams(sem):
    return pltpu.CompilerParams(dimension_semantics=sem, vmem_limit_bytes=VMEM_LIMIT)


def _rms(x, gain):
    ms = jnp.mean(x * x, axis=-1, keepdims=True)
    return x * lax.rsqrt(ms + EPS) * gain


def _nt_dot(a, b):
    return lax.dot_general(a, b, (((1,), (1,)), ((), ())), preferred_element_type=F32)


def _dot(a, b):
    return jnp.dot(a, b, preferred_element_type=F32)


def _split(x):
    hi = x.astype(BF16)
    return hi, (x - hi.astype(F32)).astype(BF16)


def _same_head(width, head_dim):
    r = lax.broadcasted_iota(jnp.int32, (width, width), 0) // head_dim
    c = lax.broadcasted_iota(jnp.int32, (width, width), 1) // head_dim
    return (r == c).astype(BF16)


def _head_stack(q, n_heads, head_dim):
    q = q.astype(F32)
    head = lax.broadcasted_iota(jnp.int32, q.shape, 1) // head_dim
    return jnp.concatenate([jnp.where(head == h, q, 0.0) for h in range(n_heads)], axis=0).astype(BF16)


def _head_unstack(o, n_heads, head_dim):
    t = o.shape[0] // n_heads
    head = lax.broadcasted_iota(jnp.int32, (t, o.shape[1]), 1) // head_dim
    out = jnp.zeros((t, o.shape[1]), o.dtype)
    for h in range(n_heads):
        out = out + jnp.where(head == h, o[h * t:(h + 1) * t], 0.0)
    return out


def _mem_kv_kernel(mem_ref, gain_ref, wt_ref, kgain_ref, k_ref, v_ref):
    h = _rms(mem_ref[0], gain_ref[0]).astype(BF16)
    mt = _nt_dot(wt_ref[0], h)
    kt = mt[:MEM_WIDTH]
    hi, lo = _split(kt * kt)
    same = _same_head(MEM_WIDTH, MEM_HEAD_DIM)
    ss = _dot(same, hi) + _dot(same, lo)
    k_ref[0, 0] = kt * lax.rsqrt(ss * (1.0 / MEM_HEAD_DIM) + EPS) * kgain_ref[0]
    v_ref[0, 0] = mt[MEM_WIDTH:]


def _mem_kv(mem, norm_mem, w_kv_t_bf, k_gain_col):
    depth = norm_mem.shape[0]
    batch, m_tok, d = mem.shape
    out = jax.ShapeDtypeStruct((depth, batch, MEM_WIDTH, m_tok), F32)
    return pl.pallas_call(
        _mem_kv_kernel,
        grid=(depth, batch),
        in_specs=[
            pl.BlockSpec((1, m_tok, d), lambda i, b: (b, 0, 0)),
            pl.BlockSpec((1, 1, d), lambda i, b: (i, 0, 0)),
            pl.BlockSpec((1, 2 * MEM_WIDTH, d), lambda i, b: (i, 0, 0)),
            pl.BlockSpec((1, MEM_WIDTH, 1), lambda i, b: (i, 0, 0)),
        ],
        out_specs=[
            pl.BlockSpec((1, 1, MEM_WIDTH, m_tok), lambda i, b: (i, b, 0, 0)),
            pl.BlockSpec((1, 1, MEM_WIDTH, m_tok), lambda i, b: (i, b, 0, 0)),
        ],
        out_shape=[out, out],
        compiler_params=_cparams(("arbitrary", "arbitrary")),
        name="mem_kv",
    )(mem, norm_mem.reshape(depth, 1, d), w_kv_t_bf, k_gain_col)


def _gmlp_in_kernel(x_ref, gain_ref, w_ref, vgain_ref, ws_ref, bst_ref, mix_ref, mq_ref, v_ref, *, seq_len, cm):
    tm = x_ref.shape[0]
    h = _rms(x_ref[...], gain_ref[...]).astype(BF16)
    proj = _dot(h, w_ref[...])
    mq_ref[...] = proj[:, 2 * MIX_WIDTH:]
    v_ref[...] = _rms(jax.nn.gelu(proj[:, MIX_WIDTH:2 * MIX_WIDTH]), vgain_ref[...])
    mix_ref[...] = jax.nn.gelu(proj[:, :MIX_WIDTH])
    r = lax.broadcasted_iota(jnp.int32, (cm, cm), 0)
    c = lax.broadcasted_iota(jnp.int32, (cm, cm), 1)
    allowed = (r // seq_len == c // seq_len) & (c % seq_len <= r % seq_len)
    if seq_len != cm:
        pick = (lax.broadcasted_iota(jnp.int32, (cm, CHUNK), 1)
                == lax.broadcasted_iota(jnp.int32, (cm, CHUNK), 0) % seq_len).astype(BF16)
    for g in range(GM_GROUPS):
        cols = slice(g * GM_GROUP_DIM, (g + 1) * GM_GROUP_DIM)
        wg = ws_ref[g].astype(BF16)
        if seq_len != cm:
            wg = _nt_dot(_dot(pick, wg).astype(BF16), pick)
        wg = jnp.where(allowed, wg, 0.0).astype(BF16)
        bias_col = jnp.concatenate([bst_ref[:seq_len, g:g + 1]] * (cm // seq_len), axis=0)
        for ci in range(tm // cm):
            rows = slice(ci * cm, (ci + 1) * cm)
            mixed = _dot(wg, v_ref[rows, cols].astype(BF16)) + bias_col
            mix_ref[rows, cols] = mix_ref[rows, cols] * mixed


def _gmlp_in(x, gain, w_bf, v_gain, ws, bs_t, *, seq_len, cm, tm, emit_v):
    n, d = x.shape
    d_in = w_bf.shape[1]
    const = lambda i: (0, 0)
    out_shape = [jax.ShapeDtypeStruct((n, MIX_WIDTH), F32), jax.ShapeDtypeStruct((n, MEM_WIDTH), F32)]
    out_specs = [pl.BlockSpec((tm, MIX_WIDTH), lambda i: (i, 0)), pl.BlockSpec((tm, MEM_WIDTH), lambda i: (i, 0))]
    scratch = []
    if emit_v:
        out_shape.append(jax.ShapeDtypeStruct((n, MIX_WIDTH), F32))
        out_specs.append(pl.BlockSpec((tm, MIX_WIDTH), lambda i: (i, 0)))
    else:
        scratch.append(pltpu.VMEM((tm, MIX_WIDTH), F32))
    return pl.pallas_call(
        functools.partial(_gmlp_in_kernel, seq_len=seq_len, cm=cm),
        grid=(n // tm,),
        in_specs=[
            pl.BlockSpec((tm, d), lambda i: (i, 0)),
            pl.BlockSpec((1, d), const),
            pl.BlockSpec((d, d_in), const),
            pl.BlockSpec((1, MIX_WIDTH), const),
            pl.BlockSpec((GM_GROUPS, CHUNK, CHUNK), lambda i: (0, 0, 0)),
            pl.BlockSpec((CHUNK, GM_GROUPS), const),
        ],
        out_specs=out_specs,
        out_shape=out_shape,
        scratch_shapes=scratch,
        compiler_params=_cparams(("arbitrary",)),
        name="gmlp_in",
    )(x, gain.reshape(1, d), w_bf, v_gain.reshape(1, MIX_WIDTH), ws, bs_t)


def _sb_in_rows_kernel(x_ref, gain_ref, w_ref, q_ref, k_ref, v_ref, mq_ref):
    h = _rms(x_ref[...], gain_ref[...]).astype(BF16)
    proj = _dot(h, w_ref[...])
    q_ref[...] = (proj[:, :MIX_WIDTH] * SB_Q_SCALE).astype(BF16)
    k_ref[...] = proj[:, MIX_WIDTH:2 * MIX_WIDTH]
    v_ref[...] = proj[:, 2 * MIX_WIDTH:3 * MIX_WIDTH]
    mq_ref[...] = proj[:, 3 * MIX_WIDTH:]


def _sb_in_rows(x, gain, w_bf):
    n, d = x.shape
    full = lambda width: pl.BlockSpec((n, width), lambda i: (0, 0))
    shp = lambda width, dt: jax.ShapeDtypeStruct((n, width), dt)
    return pl.pallas_call(
        _sb_in_rows_kernel,
        grid=(1,),
        in_specs=[full(d), pl.BlockSpec((1, d), lambda i: (0, 0)), pl.BlockSpec(w_bf.shape, lambda i: (0, 0))],
        out_specs=[full(MIX_WIDTH)] * 3 + [full(MEM_WIDTH)],
        out_shape=[shp(MIX_WIDTH, BF16), shp(MIX_WIDTH, F32), shp(MIX_WIDTH, F32), shp(MEM_WIDTH, F32)],
        compiler_params=_cparams(("arbitrary",)),
        name="sb_in_rows",
    )(x, gain.reshape(1, d), w_bf)


def _sb_in_prompt_kernel(x_ref, gain_ref, wq_ref, wkvt_ref, q_ref, kt_ref, vt_ref, ktb_ref, vtb_ref, mq_ref, *, tk):
    tm = x_ref.shape[1]
    h = _rms(x_ref[0], gain_ref[...]).astype(BF16)
    qm = _dot(h, wq_ref[...])
    q_ref[0] = (qm[:, :MIX_WIDTH] * SB_Q_SCALE).astype(BF16)
    mq_ref[0] = qm[:, MIX_WIDTH:]
    kvt = _nt_dot(wkvt_ref[...], h)
    kt_ref[0] = kvt[:MIX_WIDTH]
    vt_ref[0] = kvt[MIX_WIDTH:]
    for c in range(tm // tk):
        ktb_ref[0, c] = kvt[:MIX_WIDTH, c * tk:(c + 1) * tk].astype(BF16)
        vtb_ref[0, c] = kvt[MIX_WIDTH:, c * tk:(c + 1) * tk].astype(BF16)


def _sb_in_prompt(x, gain, wq_bf, wkvt_bf, *, tm, tk):
    b, t, d = x.shape
    const = lambda bi, i: (0, 0)
    return pl.pallas_call(
        functools.partial(_sb_in_prompt_kernel, tk=tk),
        grid=(b, t // tm),
        in_specs=[
            pl.BlockSpec((1, tm, d), lambda bi, i: (bi, i, 0)),
            pl.BlockSpec((1, d), const),
            pl.BlockSpec((d, MIX_WIDTH + MEM_WIDTH), const),
            pl.BlockSpec((2 * MIX_WIDTH, d), const),
        ],
        out_specs=[
            pl.BlockSpec((1, tm, MIX_WIDTH), lambda bi, i: (bi, i, 0)),
            pl.BlockSpec((1, MIX_WIDTH, tm), lambda bi, i: (bi, 0, i)),
            pl.BlockSpec((1, MIX_WIDTH, tm), lambda bi, i: (bi, 0, i)),
            pl.BlockSpec((1, tm // tk, MIX_WIDTH, tk), lambda bi, i: (bi, i, 0, 0)),
            pl.BlockSpec((1, tm // tk, MIX_WIDTH, tk), lambda bi, i: (bi, i, 0, 0)),
            pl.BlockSpec((1, tm, MEM_WIDTH), lambda bi, i: (bi, i, 0)),
        ],
        out_shape=[
            jax.ShapeDtypeStruct((b, t, MIX_WIDTH), BF16),
            jax.ShapeDtypeStruct((b, MIX_WIDTH, t), F32),
            jax.ShapeDtypeStruct((b, MIX_WIDTH, t), F32),
            jax.ShapeDtypeStruct((b, t // tk, MIX_WIDTH, tk), BF16),
            jax.ShapeDtypeStruct((b, t // tk, MIX_WIDTH, tk), BF16),
            jax.ShapeDtypeStruct((b, t, MEM_WIDTH), F32),
        ],
        compiler_params=_cparams(("arbitrary", "arbitrary")),
        name="sb_in_prompt",
    )(x, gain.reshape(1, d), wq_bf, wkvt_bf)


def _sb_block(q_stack, k, v, bias_col, neg_tri, carry, mask, feature_major, n_chains=1):
    tk = neg_tri.shape[0]
    n_rows = q_stack.shape[0]
    chains = [slice(c * (n_rows // n_chains), (c + 1) * (n_rows // n_chains)) for c in range(n_chains)]
    masks = [None if mask is None else mask[r] for r in chains]
    ws = [(_dot(q_stack[r], k) if feature_major else _nt_dot(q_stack[r], k)) + bias_col[r] for r in chains]
    sums = []
    for w, m in zip(ws, masks):
        sp = jnp.where(w > SOFTPLUS_LINEAR_FROM, w, jnp.log(1.0 + jnp.exp2(w)) * LOG2E)
        if m is not None:
            sp = jnp.where(m, sp, 0.0)
        sp_bf = sp.astype(BF16)
        sums.append([_dot(sp_bf[:, j * tk:(j + 1) * tk], neg_tri) for j in reversed(range(sp.shape[1] // tk))])
    contribs, carries = [], []
    for w, m, r, from_here in zip(ws, masks, chains, sums):
        c = carry[r]
        parts = []
        for i, fh in enumerate(from_here):
            j = len(from_here) - 1 - i
            parts.append(w[:, j * tk:(j + 1) * tk] + (fh + c))
            c = c + fh[:, 0:1]
        a = jnp.exp2(parts[0] if len(parts) == 1 else jnp.concatenate(parts[::-1], axis=1))
        if m is not None:
            a = jnp.where(m, a, 0.0)
        a = a.astype(BF16)
        contribs.append(_nt_dot(a, v) if feature_major else _dot(a, v))
        carries.append(c)
    if n_chains == 1:
        return contribs[0], carries[0]
    return jnp.concatenate(contribs, axis=0), jnp.concatenate(carries, axis=0)


def _neg_tri(tk):
    j = lax.broadcasted_iota(jnp.int32, (tk, tk), 0)
    s = lax.broadcasted_iota(jnp.int32, (tk, tk), 1)
    return jnp.where(j >= s, -1.0, 0.0).astype(BF16)


HEADS_PER_GROUP = 4
GROUP_WIDTH = HEADS_PER_GROUP * SB_HEAD_DIM
SB_RUN = 4
SB_CHAINS = 2


def _sb_prompt_kernel(bias_ref, q_ref, k_ref, v_ref, o_ref, qs_ref, acc_ref, carry_ref, *, tq):
    hg = pl.program_id(1)
    bias_col = jnp.concatenate(
        [jnp.full((tq, 1), bias_ref[hg * HEADS_PER_GROUP + h] * LOG2E, F32) for h in range(HEADS_PER_GROUP)], axis=0)
    tri = _neg_tri(tq)

    def query_tile(qi, _):
        q_rows = pl.ds(pl.multiple_of(qi * tq, tq), tq)
        qs_ref[...] = _head_stack(q_ref[0, q_rows, :], HEADS_PER_GROUP, SB_HEAD_DIM)
        _sb_prompt_tile(qi, k_ref, v_ref, qs_ref, acc_ref, carry_ref, bias_col, tri, tq)
        o_ref[0, q_rows, :] = _head_unstack(acc_ref[...], HEADS_PER_GROUP, SB_HEAD_DIM)
        return 0

    lax.fori_loop(0, q_ref.shape[1] // tq, query_tile, 0)


def _sb_prompt_tile(qi, k_ref, v_ref, qs_ref, acc_ref, carry_ref, bias_col, tri, tq):
    rows = HEADS_PER_GROUP * tq

    def key_run(kj, n_blocks, diagonal):
        k = jnp.concatenate([k_ref[0, kj + i] for i in range(n_blocks)], axis=1)
        v = jnp.concatenate([v_ref[0, kj + i] for i in range(n_blocks)], axis=1)
        if diagonal:
            t_row = lax.broadcasted_iota(jnp.int32, (rows, tq), 0) % tq
            mask = lax.broadcasted_iota(jnp.int32, (rows, tq), 1) < t_row
            carry = jnp.zeros((rows, 1), F32)
        else:
            mask = None
            carry = carry_ref[...]
        contrib, carry = _sb_block(qs_ref[...], k, v, bias_col, tri, carry, mask, True, n_chains=SB_CHAINS)
        if diagonal:
            acc_ref[...] = contrib
        else:
            acc_ref[...] += contrib
        carry_ref[...] = carry

    key_run(qi, 1, True)
    n_runs = qi // SB_RUN

    def single(i, _):
        key_run(qi - 1 - i, 1, False)
        return 0

    lax.fori_loop(0, qi - n_runs * SB_RUN, single, 0)

    def run(i, _):
        key_run(SB_RUN * (n_runs - 1 - i), SB_RUN, False)
        return 0

    lax.fori_loop(0, n_runs, run, 0)


def _sb_prompt(q_bf, kt_bf, vt_bf, bias, *, tq):
    b, t, w = q_bf.shape
    nkb = t // tq
    return pl.pallas_call(
        functools.partial(_sb_prompt_kernel, tq=tq),
        grid=(b, w // GROUP_WIDTH),
        in_specs=[
            pl.BlockSpec(memory_space=pltpu.SMEM),
            pl.BlockSpec((1, t, GROUP_WIDTH), lambda bi, g: (bi, 0, g)),
            pl.BlockSpec((1, nkb, GROUP_WIDTH, tq), lambda bi, g: (bi, 0, g, 0)),
            pl.BlockSpec((1, nkb, GROUP_WIDTH, tq), lambda bi, g: (bi, 0, g, 0)),
        ],
        out_specs=pl.BlockSpec((1, t, GROUP_WIDTH), lambda bi, g: (bi, 0, g)),
        out_shape=jax.ShapeDtypeStruct((b, t, w), F32),
        scratch_shapes=[pltpu.VMEM((HEADS_PER_GROUP * tq, GROUP_WIDTH), BF16),
                        pltpu.VMEM((HEADS_PER_GROUP * tq, GROUP_WIDTH), F32),
                        pltpu.VMEM((HEADS_PER_GROUP * tq, 1), F32)],
        compiler_params=_cparams(("arbitrary", "arbitrary")),
        name="sb_prompt",
    )(bias, q_bf, kt_bf, vt_bf)


PAGES_PER_STEP = 16


def _sb_decode_kernel(pt_ref, bias_ref, q_ref, kn_ref, vn_ref, *refs, t_new):
    k_pages = refs[:PAGES_PER_STEP]
    v_pages = refs[PAGES_PER_STEP:2 * PAGES_PER_STEP]
    o_ref, acc_ref, carry_ref = refs[2 * PAGES_PER_STEP:]
    j = pl.program_id(1)
    rows = SB_HEADS * t_new
    q_stack = _head_stack(q_ref[0], SB_HEADS, SB_HEAD_DIM)
    bias_col = jnp.concatenate([jnp.full((t_new, 1), bias_ref[h] * LOG2E, F32) for h in range(SB_HEADS)], axis=0)

    @pl.when(j == 0)
    def _():
        pad = jnp.zeros((PAGE_SIZE - t_new, MIX_WIDTH), F32)
        k = jnp.concatenate([kn_ref[0], pad], axis=0).astype(BF16)
        v = jnp.concatenate([vn_ref[0], pad], axis=0).astype(BF16)
        t_row = lax.broadcasted_iota(jnp.int32, (rows, PAGE_SIZE), 0) % t_new
        s_col = lax.broadcasted_iota(jnp.int32, (rows, PAGE_SIZE), 1)
        contrib, carry = _sb_block(q_stack, k, v, bias_col, _neg_tri(PAGE_SIZE),
                                   jnp.zeros((rows, 1), F32), s_col < t_row, False)
        acc_ref[...] = contrib
        carry_ref[...] = carry

    k = jnp.concatenate([r[0].astype(BF16) for r in k_pages], axis=1)
    v = jnp.concatenate([r[0].astype(BF16) for r in v_pages], axis=1)
    contrib, carry = _sb_block(q_stack, k, v, bias_col, _neg_tri(2 * PAGE_SIZE), carry_ref[...], None, True)
    acc_ref[...] += contrib
    carry_ref[...] = carry

    @pl.when(j == pl.num_programs(1) - 1)
    def _():
        o_ref[0] = _head_unstack(acc_ref[...], SB_HEADS, SB_HEAD_DIM)


def _sb_decode(q_bf, k_new, v_new, cache_kt, cache_vt, page_table, bias):
    b, t_new, w = q_bf.shape
    n_pages = page_table.shape[1]
    n_steps = n_pages // PAGES_PER_STEP

    def page_spec(slot):
        def index_map(bi, j, pt, bias):
            return (pt[bi, (n_steps - 1 - j) * PAGES_PER_STEP + slot], 0, 0)
        return pl.BlockSpec((1, w, PAGE_SIZE), index_map)

    new_spec = pl.BlockSpec((1, t_new, w), lambda bi, j, pt, bias: (bi, 0, 0))
    grid_spec = pltpu.PrefetchScalarGridSpec(
        num_scalar_prefetch=2,
        grid=(b, n_steps),
        in_specs=[new_spec, new_spec, new_spec]
        + [page_spec(s) for s in range(PAGES_PER_STEP)] * 2,
        out_specs=new_spec,
        scratch_shapes=[pltpu.VMEM((SB_HEADS * t_new, w), F32), pltpu.VMEM((SB_HEADS * t_new, 1), F32)],
    )
    return pl.pallas_call(
        functools.partial(_sb_decode_kernel, t_new=t_new),
        grid_spec=grid_spec,
        out_shape=jax.ShapeDtypeStruct((b, t_new, w), F32),
        compiler_params=_cparams(("arbitrary", "arbitrary")),
        name="sb_decode",
    )(page_table, bias, q_bf, k_new, v_new, *([cache_kt] * PAGES_PER_STEP), *([cache_vt] * PAGES_PER_STEP))


def _mem_attention_chains(qs, kt, vt, gain):
    same = _same_head(MEM_WIDTH, MEM_HEAD_DIM)
    kt = kt.astype(BF16)
    vt = vt.astype(BF16)
    sq = []
    for q in qs:
        hi, lo = _split(q * q)
        sq.append(_dot(hi, same) + _dot(lo, same))
    scores = []
    for q, ss in zip(qs, sq):
        qn = q * lax.rsqrt(ss * (1.0 / MEM_HEAD_DIM) + EPS) * gain
        q_stack = _head_stack(qn * (MEM_HEAD_DIM ** -0.5), MEM_HEADS, MEM_HEAD_DIM)
        scores.append(_dot(q_stack, kt))
    outs = []
    for s in scores:
        p = jnp.exp(s - jnp.max(s, axis=-1, keepdims=True))
        l = jnp.sum(p, axis=-1, keepdims=True)
        outs.append((_nt_dot(p.astype(BF16), vt), l))
    return [_head_unstack(o / l, MEM_HEADS, MEM_HEAD_DIM) for o, l in outs]


def _mem_attention(q, kt, vt, gain):
    return _mem_attention_chains([q], kt, vt, gain)[0]


def _mem_attend_kernel(q_ref, kt_ref, vt_ref, gain_ref, o_ref):
    for s in range(q_ref.shape[0]):
        o_ref[s] = _mem_attention(q_ref[s], kt_ref[s], vt_ref[s], gain_ref[...])


def _mem_attend(q, mkt, mvt, q_gain_full, *, layer, seqs):
    b, t, w = q.shape
    kv_spec = pl.BlockSpec((None, seqs, w, MEM_TOKENS), lambda bi: (layer, bi, 0, 0))
    return pl.pallas_call(
        _mem_attend_kernel,
        grid=(b // seqs,),
        in_specs=[
            pl.BlockSpec((seqs, t, w), lambda bi: (bi, 0, 0)),
            kv_spec,
            kv_spec,
            pl.BlockSpec((1, w), lambda bi: (0, 0)),
        ],
        out_specs=pl.BlockSpec((seqs, t, w), lambda bi: (bi, 0, 0)),
        out_shape=jax.ShapeDtypeStruct((b, t, w), F32),
        compiler_params=_cparams(("arbitrary",)),
        name="mem_attend",
    )(q, mkt, mvt, q_gain_full.reshape(1, w))


def _out_proj_kernel(mix_ref, mem_ref, x_ref, w_ref, y_ref):
    y = _dot(mix_ref[...].astype(BF16), w_ref[:MIX_WIDTH, :])
    y = y + _dot(mem_ref[...].astype(BF16), w_ref[MIX_WIDTH:, :])
    y_ref[...] = x_ref[...] + y


def _out_proj(mix, mem_out, x, w_bf, *, layer, tm):
    n, d = x.shape
    return pl.pallas_call(
        _out_proj_kernel,
        grid=(n // tm,),
        in_specs=[
            pl.BlockSpec((tm, MIX_WIDTH), lambda i: (i, 0)),
            pl.BlockSpec((tm, MEM_WIDTH), lambda i: (i, 0)),
            pl.BlockSpec((tm, d), lambda i: (i, 0)),
            pl.BlockSpec((None, d, d), lambda i: (layer, 0, 0)),
        ],
        out_specs=pl.BlockSpec((tm, d), lambda i: (i, 0)),
        out_shape=jax.ShapeDtypeStruct((n, d), F32),
        compiler_params=_cparams(("arbitrary",)),
        name="out_proj",
    )(mix, mem_out, x, w_bf)


MEM_CHAINS = 2


def _mem_out_kernel(mix_ref, mq_ref, kt_ref, vt_ref, gain_ref, x_ref, w_ref, y_ref):
    tm = x_ref.shape[1]
    chunks = [slice(c * (tm // MEM_CHAINS), (c + 1) * (tm // MEM_CHAINS)) for c in range(MEM_CHAINS)]
    ys = [_dot(mix_ref[0, r].astype(BF16), w_ref[:MIX_WIDTH, :]) for r in chunks]
    mems = _mem_attention_chains([mq_ref[0, r] for r in chunks], kt_ref[...], vt_ref[...], gain_ref[...])
    for r, y, mem in zip(chunks, ys, mems):
        y_ref[0, r] = x_ref[0, r] + (y + _dot(mem.astype(BF16), w_ref[MIX_WIDTH:, :]))


def _mem_out(mix, mq, mkt, mvt, q_gain_full, x, w_bf, *, layer, tm):
    b, t, d = x.shape
    rows = lambda width: pl.BlockSpec((1, tm, width), lambda bi, i: (bi, i, 0))
    kv_spec = pl.BlockSpec((None, None, MEM_WIDTH, MEM_TOKENS), lambda bi, i: (layer, bi, 0, 0))
    return pl.pallas_call(
        _mem_out_kernel,
        grid=(b, t // tm),
        in_specs=[rows(MIX_WIDTH), rows(MEM_WIDTH), kv_spec, kv_spec,
                  pl.BlockSpec((1, MEM_WIDTH), lambda bi, i: (0, 0)), rows(d),
                  pl.BlockSpec((None, d, d), lambda bi, i: (layer, 0, 0))],
        out_specs=rows(d),
        out_shape=jax.ShapeDtypeStruct((b, t, d), F32),
        compiler_params=_cparams(("arbitrary", "arbitrary")),
        name="mem_out",
    )(mix, mq, mkt, mvt, q_gain_full.reshape(1, MEM_WIDTH), x, w_bf)


FF_CHUNK = 256
EXT_PAD = 8
UP_AHEAD = 11


def _conv_ffn_kernel(x_ref, gain_ref, wup_ref, cw_ref, cb_ref, wdown_ref, past_ref,
                     y_ref, state_ref, ext_ref, h_ref, acc_ref):
    t = pl.program_id(1)
    tm = x_ref.shape[1]
    lo = EXT_PAD - (CONV_W - 1)

    @pl.when(t == 0)
    def _():
        ext_ref[lo:EXT_PAD, :] = past_ref[0]

    @pl.when(t > 0)
    def _():
        ext_ref[lo:EXT_PAD, :] = ext_ref[tm + lo:tm + EXT_PAD, :]

    x = x_ref[0]
    h_ref[...] = _rms(x, gain_ref[...]).astype(BF16)
    acc_ref[...] = x
    n_chunks = D_FF // FF_CHUNK

    def chunk_cols(c):
        return [slice(half * D_FF + c * FF_CHUNK, half * D_FF + (c + 1) * FF_CHUNK) for half in range(2)]

    def up_project(c):
        for cols in chunk_cols(c):
            ext_ref[EXT_PAD:EXT_PAD + tm, cols] = _dot(h_ref[...], wup_ref[:, cols])

    for c in range(min(UP_AHEAD, n_chunks)):
        up_project(c)
    for c in range(n_chunks):
        if c + UP_AHEAD < n_chunks:
            up_project(c + UP_AHEAD)
        conv = []
        for cols in chunk_cols(c):
            y = cb_ref[:, cols]
            for j in range(CONV_W):
                y = y + cw_ref[j:j + 1, cols] * ext_ref[lo + j:lo + j + tm, cols]
            conv.append(y)
        g = (jax.nn.silu(conv[0]) * conv[1]).astype(BF16)
        acc_ref[...] += _dot(g, wdown_ref[c * FF_CHUNK:(c + 1) * FF_CHUNK, :])
    y_ref[0] = acc_ref[...]
    state_ref[0] = ext_ref[tm + lo:tm + EXT_PAD, :]


def _conv_ffn(x, gain, wup_bf, cw, cb, wdown_bf, past, *, layer, tm):
    b, t, d = x.shape
    ff2 = 2 * D_FF
    const = lambda bi, i: (0, 0)
    return pl.pallas_call(
        _conv_ffn_kernel,
        grid=(b, t // tm),
        in_specs=[
            pl.BlockSpec((1, tm, d), lambda bi, i: (bi, i, 0)),
            pl.BlockSpec((1, d), const),
            pl.BlockSpec((None, d, ff2), lambda bi, i: (layer, 0, 0), pipeline_mode=pl.Buffered(1)),
            pl.BlockSpec((CONV_W, ff2), const),
            pl.BlockSpec((1, ff2), const),
            pl.BlockSpec((None, D_FF, d), lambda bi, i: (layer, 0, 0), pipeline_mode=pl.Buffered(1)),
            pl.BlockSpec((1, CONV_W - 1, ff2), lambda bi, i: (bi, 0, 0)),
        ],
        out_specs=[
            pl.BlockSpec((1, tm, d), lambda bi, i: (bi, i, 0)),
            pl.BlockSpec((1, CONV_W - 1, ff2), lambda bi, i: (bi, 0, 0)),
        ],
        out_shape=[jax.ShapeDtypeStruct((b, t, d), F32), jax.ShapeDtypeStruct((b, CONV_W - 1, ff2), F32)],
        scratch_shapes=[pltpu.VMEM((EXT_PAD + tm, ff2), F32), pltpu.VMEM((tm, d), BF16), pltpu.VMEM((tm, d), F32)],
        compiler_params=_cparams(("arbitrary", "arbitrary")),
        name="conv_ffn",
    )(x, gain.reshape(1, d), wup_bf, cw, cb.reshape(1, ff2), wdown_bf, past)


def _conv_ffn_rows_kernel(x_ref, gain_ref, wup_ref, cw_ref, cb_ref, wdown_ref, prev1_ref, prev2_ref,
                          y_ref, up_ref, h_ref, acc_ref, *, seq_len):
    n = x_ref.shape[0]
    x = x_ref[...]
    h_ref[...] = _rms(x, gain_ref[...]).astype(BF16)
    acc_ref[...] = x
    t_idx = lax.broadcasted_iota(jnp.int32, (n, FF_CHUNK), 0) % seq_len
    n_chunks = D_FF // FF_CHUNK
    for c in range(n_chunks):
        for half in range(2):
            cols = slice(half * D_FF + c * FF_CHUNK, half * D_FF + (c + 1) * FF_CHUNK)
            up_ref[:, cols] = _dot(h_ref[...], wup_ref[:, cols])
    for c in range(n_chunks):
        conv = []
        for half in range(2):
            cols = slice(half * D_FF + c * FF_CHUNK, half * D_FF + (c + 1) * FF_CHUNK)
            up = up_ref[:, cols]
            back1 = jnp.where(t_idx >= 1, pltpu.roll(up, 1, 0), prev1_ref[:, cols])
            back2 = jnp.where(t_idx >= 2, pltpu.roll(up, 2, 0), prev2_ref[:, cols])
            conv.append(cb_ref[:, cols] + cw_ref[2:3, cols] * up + cw_ref[1:2, cols] * back1
                        + cw_ref[0:1, cols] * back2)
        g = (jax.nn.silu(conv[0]) * conv[1]).astype(BF16)
        acc_ref[...] += _dot(g, wdown_ref[c * FF_CHUNK:(c + 1) * FF_CHUNK, :])
    y_ref[...] = acc_ref[...]


def _conv_ffn_rows(x, gain, wup_bf, cw, cb, wdown_bf, past, *, layer, seq_len):
    n, d = x.shape
    ff2 = 2 * D_FF
    b = n // seq_len
    prev2 = jnp.pad(past, ((0, 0), (0, seq_len - 2), (0, 0))).reshape(n, ff2)
    prev1 = jnp.pad(past[:, 1:], ((0, 0), (0, seq_len - 1), (0, 0))).reshape(n, ff2)
    tm = 128
    const = lambda i: (0, 0)
    rows = lambda width: pl.BlockSpec((tm, width), lambda i: (i, 0))
    y, up = pl.pallas_call(
        functools.partial(_conv_ffn_rows_kernel, seq_len=seq_len),
        grid=(n // tm,),
        in_specs=[rows(d), pl.BlockSpec((1, d), const),
                  pl.BlockSpec((None, d, ff2), lambda i: (layer, 0, 0), pipeline_mode=pl.Buffered(1)),
                  pl.BlockSpec((CONV_W, ff2), const), pl.BlockSpec((1, ff2), const),
                  pl.BlockSpec((None, D_FF, d), lambda i: (layer, 0, 0), pipeline_mode=pl.Buffered(1)),
                  rows(ff2), rows(ff2)],
        out_specs=[rows(d), rows(ff2)],
        out_shape=[jax.ShapeDtypeStruct((n, d), F32), jax.ShapeDtypeStruct((n, ff2), F32)],
        scratch_shapes=[pltpu.VMEM((tm, d), BF16), pltpu.VMEM((tm, d), F32)],
        compiler_params=_cparams(("arbitrary",)),
        name="conv_ffn_rows",
    )(x, gain.reshape(1, d), wup_bf, cw, cb.reshape(1, ff2), wdown_bf, prev1, prev2)
    return y, up.reshape(b, seq_len, ff2)[:, seq_len - (CONV_W - 1):]


def _feature_major(x):
    lead = x.shape[:-3]
    n, h, dh = x.shape[-3:]
    nd = len(lead)
    return jnp.transpose(x, (*range(nd), nd + 1, nd + 2, nd)).reshape(*lead, h * dh, n)


def _position_major(xt, n_heads):
    lead = xt.shape[:-2]
    w, n = xt.shape[-2:]
    nd = len(lead)
    x4 = xt.reshape(*lead, n_heads, w // n_heads, n)
    return jnp.transpose(x4, (*range(nd), nd + 2, nd, nd + 1))


def kernel(x_prompt, x_sample, cache_mem_k, cache_mem_v, cache_sb_k, cache_sb_v, state_ffn_conv, page_table, mem_prompt, norm_mix, norm_ffn, norm_mem, w_in_a, gm_v_norm, gm_ws, gm_bs, w_in_b, sb_bias, w_mem_kv, mem_q_norm, mem_k_norm, w_out, w_up, conv_w, conv_b, w_down):
    depth = norm_mix.shape[0]
    bp, tp, d = x_prompt.shape
    bs_, ts, _ = x_sample.shape
    n_p, n_s = bp * tp, bs_ * ts
    sb_tile = 256

    k_gain_col = jnp.tile(mem_k_norm, (1, MEM_HEADS)).reshape(depth, MEM_WIDTH, 1)
    mem_kt_p, mem_vt_p = _mem_kv(mem_prompt, norm_mem, jnp.swapaxes(w_mem_kv, 1, 2).astype(BF16), k_gain_col)
    q_gain_full = jnp.tile(mem_q_norm, (1, MEM_HEADS))
    mem_kt_s = _feature_major(cache_mem_k)
    mem_vt_s = _feature_major(cache_mem_v)

    w_out_bf = w_out.astype(BF16)
    w_up_bf = w_up.astype(BF16)
    w_down_bf = w_down.astype(BF16)

    xp = x_prompt.reshape(n_p, d)
    xs = x_sample.reshape(n_s, d)
    sb_k_p, sb_v_p, sb_k_s, sb_v_s, gm_v_s, conv_p, conv_s = [], [], [], [], [], [], []
    for i in range(depth):
        if i % 2 == 0:
            a = i // 2
            w_bf = w_in_a[a].astype(BF16)
            bs_t = gm_bs[a].T
            mix_p, mq_p = _gmlp_in(xp, norm_mix[i], w_bf, gm_v_norm[a], gm_ws[a], bs_t,
                                   seq_len=CHUNK, cm=CHUNK, tm=512, emit_v=False)
            mix_s, mq_s, v_new = _gmlp_in(xs, norm_mix[i], w_bf, gm_v_norm[a], gm_ws[a], bs_t,
                                          seq_len=ts, cm=n_s, tm=n_s, emit_v=True)
            gm_v_s.append(v_new.reshape(bs_, ts, MIX_WIDTH))
        else:
            b = i // 2
            w = w_in_b[b]
            wq_bf = jnp.concatenate([w[:, :MIX_WIDTH], w[:, 3 * MIX_WIDTH:]], axis=1).astype(BF16)
            wkvt_bf = w[:, MIX_WIDTH:3 * MIX_WIDTH].T.astype(BF16)
            q_p, kt_p, vt_p, ktb_p, vtb_p, mq_p = _sb_in_prompt(
                xp.reshape(bp, tp, d), norm_mix[i], wq_bf, wkvt_bf, tm=512, tk=sb_tile)
            mq_p = mq_p.reshape(n_p, MEM_WIDTH)
            q_s, k_s, v_s, mq_s = _sb_in_rows(xs, norm_mix[i], w.astype(BF16))
            mix_p = _sb_prompt(q_p, ktb_p, vtb_p, sb_bias[b], tq=sb_tile).reshape(n_p, MIX_WIDTH)
            mix_s = _sb_decode(q_s.reshape(bs_, ts, MIX_WIDTH), k_s.reshape(bs_, ts, MIX_WIDTH),
                               v_s.reshape(bs_, ts, MIX_WIDTH),
                               _feature_major(cache_sb_k[b]), _feature_major(cache_sb_v[b]),
                               page_table, sb_bias[b]).reshape(n_s, MIX_WIDTH)
            sb_k_p.append(_position_major(kt_p, SB_HEADS))
            sb_v_p.append(_position_major(vt_p, SB_HEADS))
            sb_k_s.append(k_s.reshape(bs_, ts, SB_HEADS, SB_HEAD_DIM))
            sb_v_s.append(v_s.reshape(bs_, ts, SB_HEADS, SB_HEAD_DIM))
        xp3 = _mem_out(mix_p.reshape(bp, tp, MIX_WIDTH), mq_p.reshape(bp, tp, MEM_WIDTH), mem_kt_p, mem_vt_p,
                       q_gain_full[i], xp.reshape(bp, tp, d), w_out_bf, layer=i, tm=1024)
        mem_out_s = _mem_attend(mq_s.reshape(bs_, ts, MEM_WIDTH), mem_kt_s, mem_vt_s, q_gain_full[i],
                                layer=i, seqs=8)
        xs = _out_proj(mix_s, mem_out_s.reshape(n_s, MEM_WIDTH), xs, w_out_bf, layer=i, tm=n_s)
        zero_rows = jnp.zeros((bp, CONV_W - 1, 2 * D_FF), F32)
        xp3, cp = _conv_ffn(xp3, norm_ffn[i], w_up_bf, conv_w[i], conv_b[i], w_down_bf,
                            zero_rows, layer=i, tm=512)
        xs, cs = _conv_ffn_rows(xs, norm_ffn[i], w_up_bf, conv_w[i], conv_b[i], w_down_bf,
                                state_ffn_conv[i], layer=i, seq_len=ts)
        xp = xp3.reshape(n_p, d)
        conv_p.append(cp)
        conv_s.append(cs)

    return (xp.reshape(bp, tp, d), xs.reshape(bs_, ts, d),
            _position_major(mem_kt_p, MEM_HEADS), _position_major(mem_vt_p, MEM_HEADS),
            jnp.stack(sb_k_p), jnp.stack(sb_v_p), jnp.stack(sb_k_s), jnp.stack(sb_v_s),
            jnp.stack(gm_v_s), jnp.stack(conv_p), jnp.stack(conv_s))
```

```python
import functools

import jax
import jax.numpy as jnp
from jax import lax
from jax.experimental import pallas as pl
from jax.experimental.pallas import tpu as pltpu

D_MODEL = 1024
MIX_WIDTH = 768
MEM_WIDTH = 256
CHUNK = 128
GM_GROUP_DIM = 128
GM_GROUPS = 6
SB_HEAD_DIM = 64
SB_HEADS = 12
MEM_TOKENS = 256
MEM_HEADS = 4
MEM_HEAD_DIM = 64
D_FF = 2816
CONV_W = 3
PAGE_SIZE = 128
EPS = 1e-6

VMEM_LIMIT = 56 * 1024 * 1024
PROJ_TILE = 512
MEM_OUT_TILE = 1024
FFN_TILE = 512
FFN_SAMPLE_TILE = 128
SB_TILE = 256
MEM_SEQS_PER_STEP = 8
F32 = jnp.float32
BF16 = jnp.bfloat16
LOG2E = 1.4426950408889634
SB_Q_SCALE = SB_HEAD_DIM ** -0.5 * LOG2E
SOFTPLUS_LINEAR_FROM = 64.0


def _cparams(sem):
    return pltpu.CompilerParams(dimension_semantics=sem, vmem_limit_bytes=VMEM_LIMIT)


def _rms(x, gain):
    ms = jnp.mean(x * x, axis=-1, keepdims=True)
    return x * lax.rsqrt(ms + EPS) * gain


def _nt_dot(a, b):
    return lax.dot_general(a, b, (((1,), (1,)), ((), ())), preferred_element_type=F32)


def _dot(a, b):
    return jnp.dot(a, b, preferred_element_type=F32)


def _split(x):
    hi = x.astype(BF16)
    return hi, (x - hi.astype(F32)).astype(BF16)


def _same_head(width, head_dim):
    r = lax.broadcasted_iota(jnp.int32, (width, width), 0) // head_dim
    c = lax.broadcasted_iota(jnp.int32, (width, width), 1) // head_dim
    return (r == c).astype(BF16)


def _head_stack(q, n_heads, head_dim):
    q = q.astype(F32)
    head = lax.broadcasted_iota(jnp.int32, q.shape, 1) // head_dim
    return jnp.concatenate([jnp.where(head == h, q, 0.0) for h in range(n_heads)], axis=0).astype(BF16)


def _head_unstack(o, n_heads, head_dim):
    t = o.shape[0] // n_heads
    head = lax.broadcasted_iota(jnp.int32, (t, o.shape[1]), 1) // head_dim
    out = jnp.zeros((t, o.shape[1]), o.dtype)
    for h in range(n_heads):
        out = out + jnp.where(head == h, o[h * t:(h + 1) * t], 0.0)
    return out


def _mem_kv_kernel(mem_ref, gain_ref, wt_ref, kgain_ref, k_ref, v_ref):
    h = _rms(mem_ref[0], gain_ref[0]).astype(BF16)
    mt = _nt_dot(wt_ref[0], h)
    kt = mt[:MEM_WIDTH]
    hi, lo = _split(kt * kt)
    same = _same_head(MEM_WIDTH, MEM_HEAD_DIM)
    ss = _dot(same, hi) + _dot(same, lo)
    k_ref[0, 0] = kt * lax.rsqrt(ss * (1.0 / MEM_HEAD_DIM) + EPS) * kgain_ref[0]
    v_ref[0, 0] = mt[MEM_WIDTH:]


def _mem_kv(mem, norm_mem, w_kv_t_bf, k_gain_col):
    depth = norm_mem.shape[0]
    batch, m_tok, d = mem.shape
    out = jax.ShapeDtypeStruct((depth, batch, MEM_WIDTH, m_tok), F32)
    return pl.pallas_call(
        _mem_kv_kernel,
        grid=(depth, batch),
        in_specs=[
            pl.BlockSpec((1, m_tok, d), lambda i, b: (b, 0, 0)),
            pl.BlockSpec((1, 1, d), lambda i, b: (i, 0, 0)),
            pl.BlockSpec((1, 2 * MEM_WIDTH, d), lambda i, b: (i, 0, 0)),
            pl.BlockSpec((1, MEM_WIDTH, 1), lambda i, b: (i, 0, 0)),
        ],
        out_specs=[
            pl.BlockSpec((1, 1, MEM_WIDTH, m_tok), lambda i, b: (i, b, 0, 0)),
            pl.BlockSpec((1, 1, MEM_WIDTH, m_tok), lambda i, b: (i, b, 0, 0)),
        ],
        out_shape=[out, out],
        compiler_params=_cparams(("arbitrary", "arbitrary")),
        name="mem_kv",
    )(mem, norm_mem.reshape(depth, 1, d), w_kv_t_bf, k_gain_col)


def _gmlp_in_kernel(x_ref, gain_ref, w_ref, vgain_ref, ws_ref, bst_ref, mix_ref, mq_ref, v_ref, *, seq_len, cm):
    tm = x_ref.shape[0]
    h = _rms(x_ref[...], gain_ref[...]).astype(BF16)
    proj = _dot(h, w_ref[...])
    mq_ref[...] = proj[:, 2 * MIX_WIDTH:]
    v_ref[...] = _rms(jax.nn.gelu(proj[:, MIX_WIDTH:2 * MIX_WIDTH]), vgain_ref[...])
    mix_ref[...] = jax.nn.gelu(proj[:, :MIX_WIDTH])
    r = lax.broadcasted_iota(jnp.int32, (cm, cm), 0)
    c = lax.broadcasted_iota(jnp.int32, (cm, cm), 1)
    allowed = (r // seq_len == c // seq_len) & (c % seq_len <= r % seq_len)
    if seq_len != cm:
        pick = (lax.broadcasted_iota(jnp.int32, (cm, CHUNK), 1)
                == lax.broadcasted_iota(jnp.int32, (cm, CHUNK), 0) % seq_len).astype(BF16)
    for g in range(GM_GROUPS):
        cols = slice(g * GM_GROUP_DIM, (g + 1) * GM_GROUP_DIM)
        wg = ws_ref[g].astype(BF16)
        if seq_len != cm:
            wg = _nt_dot(_dot(pick, wg).astype(BF16), pick)
        wg = jnp.where(allowed, wg, 0.0).astype(BF16)
        bias_col = jnp.concatenate([bst_ref[:seq_len, g:g + 1]] * (cm // seq_len), axis=0)
        for ci in range(tm // cm):
            rows = slice(ci * cm, (ci + 1) * cm)
            mixed = _dot(wg, v_ref[rows, cols].astype(BF16)) + bias_col
            mix_ref[rows, cols] = mix_ref[rows, cols] * mixed


def _gmlp_in(x, gain, w_bf, v_gain, ws, bs_t, *, seq_len, cm, tm, emit_v):
    n, d = x.shape
    d_in = w_bf.shape[1]
    const = lambda i: (0, 0)
    out_shape = [jax.ShapeDtypeStruct((n, MIX_WIDTH), F32), jax.ShapeDtypeStruct((n, MEM_WIDTH), F32)]
    out_specs = [pl.BlockSpec((tm, MIX_WIDTH), lambda i: (i, 0)), pl.BlockSpec((tm, MEM_WIDTH), lambda i: (i, 0))]
    scratch = []
    if emit_v:
        out_shape.append(jax.ShapeDtypeStruct((n, MIX_WIDTH), F32))
        out_specs.append(pl.BlockSpec((tm, MIX_WIDTH), lambda i: (i, 0)))
    else:
        scratch.append(pltpu.VMEM((tm, MIX_WIDTH), F32))
    return pl.pallas_call(
        functools.partial(_gmlp_in_kernel, seq_len=seq_len, cm=cm),
        grid=(n // tm,),
        in_specs=[
            pl.BlockSpec((tm, d), lambda i: (i, 0)),
            pl.BlockSpec((1, d), const),
            pl.BlockSpec((d, d_in), const),
            pl.BlockSpec((1, MIX_WIDTH), const),
            pl.BlockSpec((GM_GROUPS, CHUNK, CHUNK), lambda i: (0, 0, 0)),
            pl.BlockSpec((CHUNK, GM_GROUPS), const),
        ],
        out_specs=out_specs,
        out_shape=out_shape,
        scratch_shapes=scratch,
        compiler_params=_cparams(("arbitrary",)),
        name="gmlp_in",
    )(x, gain.reshape(1, d), w_bf, v_gain.reshape(1, MIX_WIDTH), ws, bs_t)


def _sb_in_rows_kernel(x_ref, gain_ref, w_ref, q_ref, k_ref, v_ref, mq_ref):
    h = _rms(x_ref[...], gain_ref[...]).astype(BF16)
    proj = _dot(h, w_ref[...])
    q_ref[...] = (proj[:, :MIX_WIDTH] * SB_Q_SCALE).astype(BF16)
    k_ref[...] = proj[:, MIX_WIDTH:2 * MIX_WIDTH]
    v_ref[...] = proj[:, 2 * MIX_WIDTH:3 * MIX_WIDTH]
    mq_ref[...] = proj[:, 3 * MIX_WIDTH:]


def _sb_in_rows(x, gain, w_bf):
    n, d = x.shape
    full = lambda width: pl.BlockSpec((n, width), lambda i: (0, 0))
    shp = lambda width, dt: jax.ShapeDtypeStruct((n, width), dt)
    return pl.pallas_call(
        _sb_in_rows_kernel,
        grid=(1,),
        in_specs=[full(d), pl.BlockSpec((1, d), lambda i: (0, 0)), pl.BlockSpec(w_bf.shape, lambda i: (0, 0))],
        out_specs=[full(MIX_WIDTH)] * 3 + [full(MEM_WIDTH)],
        out_shape=[shp(MIX_WIDTH, BF16), shp(MIX_WIDTH, F32), shp(MIX_WIDTH, F32), shp(MEM_WIDTH, F32)],
        compiler_params=_cparams(("arbitrary",)),
        name="sb_in_rows",
    )(x, gain.reshape(1, d), w_bf)


def _sb_in_prompt_kernel(x_ref, gain_ref, wq_ref, wkvt_ref, q_ref, kt_ref, vt_ref, ktb_ref, vtb_ref, mq_ref, *, tk):
    tm = x_ref.shape[1]
    h = _rms(x_ref[0], gain_ref[...]).astype(BF16)
    qm = _dot(h, wq_ref[...])
    q_ref[0] = (qm[:, :MIX_WIDTH] * SB_Q_SCALE).astype(BF16)
    mq_ref[0] = qm[:, MIX_WIDTH:]
    kvt = _nt_dot(wkvt_ref[...], h)
    kt_ref[0] = kvt[:MIX_WIDTH]
    vt_ref[0] = kvt[MIX_WIDTH:]
    for c in range(tm // tk):
        ktb_ref[0, c] = kvt[:MIX_WIDTH, c * tk:(c + 1) * tk].astype(BF16)
        vtb_ref[0, c] = kvt[MIX_WIDTH:, c * tk:(c + 1) * tk].astype(BF16)


def _sb_in_prompt(x, gain, wq_bf, wkvt_bf, *, tm, tk):
    b, t, d = x.shape
    const = lambda bi, i: (0, 0)
    return pl.pallas_call(
        functools.partial(_sb_in_prompt_kernel, tk=tk),
        grid=(b, t // tm),
        in_specs=[
            pl.BlockSpec((1, tm, d), lambda bi, i: (bi, i, 0)),
            pl.BlockSpec((1, d), const),
            pl.BlockSpec((d, MIX_WIDTH + MEM_WIDTH), const),
            pl.BlockSpec((2 * MIX_WIDTH, d), const),
        ],
        out_specs=[
            pl.BlockSpec((1, tm, MIX_WIDTH), lambda bi, i: (bi, i, 0)),
            pl.BlockSpec((1, MIX_WIDTH, tm), lambda bi, i: (bi, 0, i)),
            pl.BlockSpec((1, MIX_WIDTH, tm), lambda bi, i: (bi, 0, i)),
            pl.BlockSpec((1, tm // tk, MIX_WIDTH, tk), lambda bi, i: (bi, i, 0, 0)),
            pl.BlockSpec((1, tm // tk, MIX_WIDTH, tk), lambda bi, i: (bi, i, 0, 0)),
            pl.BlockSpec((1, tm, MEM_WIDTH), lambda bi, i: (bi, i, 0)),
        ],
        out_shape=[
            jax.ShapeDtypeStruct((b, t, MIX_WIDTH), BF16),
            jax.ShapeDtypeStruct((b, MIX_WIDTH, t), F32),
            jax.ShapeDtypeStruct((b, MIX_WIDTH, t), F32),
            jax.ShapeDtypeStruct((b, t // tk, MIX_WIDTH, tk), BF16),
            jax.ShapeDtypeStruct((b, t // tk, MIX_WIDTH, tk), BF16),
            jax.ShapeDtypeStruct((b, t, MEM_WIDTH), F32),
        ],
        compiler_params=_cparams(("arbitrary", "arbitrary")),
        name="sb_in_prompt",
    )(x, gain.reshape(1, d), wq_bf, wkvt_bf)


def _sb_block(q_stack, k, v, bias_col, neg_tri, carry, mask, feature_major, n_chains=1):
    tk = neg_tri.shape[0]
    n_rows = q_stack.shape[0]
    chains = [slice(c * (n_rows // n_chains), (c + 1) * (n_rows // n_chains)) for c in range(n_chains)]
    masks = [None if mask is None else mask[r] for r in chains]
    ws = [(_dot(q_stack[r], k) if feature_major else _nt_dot(q_stack[r], k)) + bias_col[r] for r in chains]
    sums = []
    for w, m in zip(ws, masks):
        sp = jnp.where(w > SOFTPLUS_LINEAR_FROM, w, jnp.log(1.0 + jnp.exp2(w)) * LOG2E)
        if m is not None:
            sp = jnp.where(m, sp, 0.0)
        sp_bf = sp.astype(BF16)
        sums.append([_dot(sp_bf[:, j * tk:(j + 1) * tk], neg_tri) for j in reversed(range(sp.shape[1] // tk))])
    contribs, carries = [], []
    for w, m, r, from_here in zip(ws, masks, chains, sums):
        c = carry[r]
        parts = []
        for i, fh in enumerate(from_here):
            j = len(from_here) - 1 - i
            parts.append(w[:, j * tk:(j + 1) * tk] + (fh + c))
            c = c + fh[:, 0:1]
        a = jnp.exp2(parts[0] if len(parts) == 1 else jnp.concatenate(parts[::-1], axis=1))
        if m is not None:
            a = jnp.where(m, a, 0.0)
        a = a.astype(BF16)
        contribs.append(_nt_dot(a, v) if feature_major else _dot(a, v))
        carries.append(c)
    if n_chains == 1:
        return contribs[0], carries[0]
    return jnp.concatenate(contribs, axis=0), jnp.concatenate(carries, axis=0)


def _neg_tri(tk):
    j = lax.broadcasted_iota(jnp.int32, (tk, tk), 0)
    s = lax.broadcasted_iota(jnp.int32, (tk, tk), 1)
    return jnp.where(j >= s, -1.0, 0.0).astype(BF16)


HEADS_PER_GROUP = 4
GROUP_WIDTH = HEADS_PER_GROUP * SB_HEAD_DIM
SB_RUN = 4
SB_CHAINS = 2


def _sb_prompt_kernel(bias_ref, q_ref, k_ref, v_ref, o_ref, qs_ref, acc_ref, carry_ref, *, tq):
    hg = pl.program_id(1)
    bias_col = jnp.concatenate(
        [jnp.full((tq, 1), bias_ref[hg * HEADS_PER_GROUP + h] * LOG2E, F32) for h in range(HEADS_PER_GROUP)], axis=0)
    tri = _neg_tri(tq)

    def query_tile(qi, _):
        q_rows = pl.ds(pl.multiple_of(qi * tq, tq), tq)
        qs_ref[...] = _head_stack(q_ref[0, q_rows, :], HEADS_PER_GROUP, SB_HEAD_DIM)
        _sb_prompt_tile(qi, k_ref, v_ref, qs_ref, acc_ref, carry_ref, bias_col, tri, tq)
        o_ref[0, q_rows, :] = _head_unstack(acc_ref[...], HEADS_PER_GROUP, SB_HEAD_DIM)
        return 0

    lax.fori_loop(0, q_ref.shape[1] // tq, query_tile, 0)


def _sb_prompt_tile(qi, k_ref, v_ref, qs_ref, acc_ref, carry_ref, bias_col, tri, tq):
    rows = HEADS_PER_GROUP * tq

    def key_run(kj, n_blocks, diagonal):
        k = jnp.concatenate([k_ref[0, kj + i] for i in range(n_blocks)], axis=1)
        v = jnp.concatenate([v_ref[0, kj + i] for i in range(n_blocks)], axis=1)
        if diagonal:
            t_row = lax.broadcasted_iota(jnp.int32, (rows, tq), 0) % tq
            mask = lax.broadcasted_iota(jnp.int32, (rows, tq), 1) < t_row
            carry = jnp.zeros((rows, 1), F32)
        else:
            mask = None
            carry = carry_ref[...]
        contrib, carry = _sb_block(qs_ref[...], k, v, bias_col, tri, carry, mask, True, n_chains=SB_CHAINS)
        if diagonal:
            acc_ref[...] = contrib
        else:
            acc_ref[...] += contrib
        carry_ref[...] = carry

    key_run(qi, 1, True)
    n_runs = qi // SB_RUN
    top = qi
    size = 1
    while size < SB_RUN:
        has_run = (qi // size) % 2 == 1

        @pl.when(has_run)
        def _(top=top, size=size):
            key_run(top - size, size, False)

        top = top - jnp.where(has_run, size, 0)
        size *= 2

    def run(i, _):
        key_run(SB_RUN * (n_runs - 1 - i), SB_RUN, False)
        return 0

    lax.fori_loop(0, n_runs, run, 0)


def _sb_prompt(q_bf, kt_bf, vt_bf, bias, *, tq):
    b, t, w = q_bf.shape
    nkb = t // tq
    return pl.pallas_call(
        functools.partial(_sb_prompt_kernel, tq=tq),
        grid=(b, w // GROUP_WIDTH),
        in_specs=[
            pl.BlockSpec(memory_space=pltpu.SMEM),
            pl.BlockSpec((1, t, GROUP_WIDTH), lambda bi, g: (bi, 0, g)),
            pl.BlockSpec((1, nkb, GROUP_WIDTH, tq), lambda bi, g: (bi, 0, g, 0)),
            pl.BlockSpec((1, nkb, GROUP_WIDTH, tq), lambda bi, g: (bi, 0, g, 0)),
        ],
        out_specs=pl.BlockSpec((1, t, GROUP_WIDTH), lambda bi, g: (bi, 0, g)),
        out_shape=jax.ShapeDtypeStruct((b, t, w), F32),
        scratch_shapes=[pltpu.VMEM((HEADS_PER_GROUP * tq, GROUP_WIDTH), BF16),
                        pltpu.VMEM((HEADS_PER_GROUP * tq, GROUP_WIDTH), F32),
                        pltpu.VMEM((HEADS_PER_GROUP * tq, 1), F32)],
        compiler_params=_cparams(("arbitrary", "arbitrary")),
        name="sb_prompt",
    )(bias, q_bf, kt_bf, vt_bf)


PAGES_PER_STEP = 16


def _sb_decode_kernel(pt_ref, bias_ref, q_ref, kn_ref, vn_ref, *refs, t_new):
    k_pages = refs[:PAGES_PER_STEP]
    v_pages = refs[PAGES_PER_STEP:2 * PAGES_PER_STEP]
    o_ref, acc_ref, carry_ref = refs[2 * PAGES_PER_STEP:]
    j = pl.program_id(1)
    rows = SB_HEADS * t_new
    q_stack = _head_stack(q_ref[0], SB_HEADS, SB_HEAD_DIM)
    bias_col = jnp.concatenate([jnp.full((t_new, 1), bias_ref[h] * LOG2E, F32) for h in range(SB_HEADS)], axis=0)

    @pl.when(j == 0)
    def _():
        pad = jnp.zeros((PAGE_SIZE - t_new, MIX_WIDTH), F32)
        k = jnp.concatenate([kn_ref[0], pad], axis=0).astype(BF16)
        v = jnp.concatenate([vn_ref[0], pad], axis=0).astype(BF16)
        t_row = lax.broadcasted_iota(jnp.int32, (rows, PAGE_SIZE), 0) % t_new
        s_col = lax.broadcasted_iota(jnp.int32, (rows, PAGE_SIZE), 1)
        contrib, carry = _sb_block(q_stack, k, v, bias_col, _neg_tri(PAGE_SIZE),
                                   jnp.zeros((rows, 1), F32), s_col < t_row, False)
        acc_ref[...] = contrib
        carry_ref[...] = carry

    k = jnp.concatenate([r[0].astype(BF16) for r in k_pages], axis=1)
    v = jnp.concatenate([r[0].astype(BF16) for r in v_pages], axis=1)
    contrib, carry = _sb_block(q_stack, k, v, bias_col, _neg_tri(2 * PAGE_SIZE), carry_ref[...], None, True)
    acc_ref[...] += contrib
    carry_ref[...] = carry

    @pl.when(j == pl.num_programs(1) - 1)
    def _():
        o_ref[0] = _head_unstack(acc_ref[...], SB_HEADS, SB_HEAD_DIM)


def _sb_decode(q_bf, k_new, v_new, cache_kt, cache_vt, page_table, bias):
    b, t_new, w = q_bf.shape
    n_pages = page_table.shape[1]
    n_steps = n_pages // PAGES_PER_STEP

    def page_spec(slot):
        def index_map(bi, j, pt, bias):
            return (pt[bi, (n_steps - 1 - j) * PAGES_PER_STEP + slot], 0, 0)
        return pl.BlockSpec((1, w, PAGE_SIZE), index_map)

    new_spec = pl.BlockSpec((1, t_new, w), lambda bi, j, pt, bias: (bi, 0, 0))
    grid_spec = pltpu.PrefetchScalarGridSpec(
        num_scalar_prefetch=2,
        grid=(b, n_steps),
        in_specs=[new_spec, new_spec, new_spec]
        + [page_spec(s) for s in range(PAGES_PER_STEP)] * 2,
        out_specs=new_spec,
        scratch_shapes=[pltpu.VMEM((SB_HEADS * t_new, w), F32), pltpu.VMEM((SB_HEADS * t_new, 1), F32)],
    )
    return pl.pallas_call(
        functools.partial(_sb_decode_kernel, t_new=t_new),
        grid_spec=grid_spec,
        out_shape=jax.ShapeDtypeStruct((b, t_new, w), F32),
        compiler_params=_cparams(("arbitrary", "arbitrary")),
        name="sb_decode",
    )(page_table, bias, q_bf, k_new, v_new, *([cache_kt] * PAGES_PER_STEP), *([cache_vt] * PAGES_PER_STEP))


def _mem_attention_chains(qs, kt, vt, gain):
    same = _same_head(MEM_WIDTH, MEM_HEAD_DIM)
    kt = kt.astype(BF16)
    vt = vt.astype(BF16)
    sq = []
    for q in qs:
        hi, lo = _split(q * q)
        sq.append(_dot(hi, same) + _dot(lo, same))
    scores = []
    for q, ss in zip(qs, sq):
        qn = q * lax.rsqrt(ss * (1.0 / MEM_HEAD_DIM) + EPS) * gain
        q_stack = _head_stack(qn * (MEM_HEAD_DIM ** -0.5), MEM_HEADS, MEM_HEAD_DIM)
        scores.append(_dot(q_stack, kt))
    outs = []
    for s in scores:
        p = jnp.exp(s - jnp.max(s, axis=-1, keepdims=True))
        l = jnp.sum(p, axis=-1, keepdims=True)
        outs.append((_nt_dot(p.astype(BF16), vt), l))
    return [_head_unstack(o / l, MEM_HEADS, MEM_HEAD_DIM) for o, l in outs]


def _mem_attention(q, kt, vt, gain):
    return _mem_attention_chains([q], kt, vt, gain)[0]


def _mem_attend_kernel(q_ref, kt_ref, vt_ref, gain_ref, o_ref):
    for s in range(q_ref.shape[0]):
        o_ref[s] = _mem_attention(q_ref[s], kt_ref[s], vt_ref[s], gain_ref[...])


def _mem_attend(q, mkt, mvt, q_gain_full, *, layer, seqs):
    b, t, w = q.shape
    kv_spec = pl.BlockSpec((None, seqs, w, MEM_TOKENS), lambda bi: (layer, bi, 0, 0))
    return pl.pallas_call(
        _mem_attend_kernel,
        grid=(b // seqs,),
        in_specs=[
            pl.BlockSpec((seqs, t, w), lambda bi: (bi, 0, 0)),
            kv_spec,
            kv_spec,
            pl.BlockSpec((1, w), lambda bi: (0, 0)),
        ],
        out_specs=pl.BlockSpec((seqs, t, w), lambda bi: (bi, 0, 0)),
        out_shape=jax.ShapeDtypeStruct((b, t, w), F32),
        compiler_params=_cparams(("arbitrary",)),
        name="mem_attend",
    )(q, mkt, mvt, q_gain_full.reshape(1, w))


def _out_proj_kernel(mix_ref, mem_ref, x_ref, w_ref, y_ref):
    y = _dot(mix_ref[...].astype(BF16), w_ref[:MIX_WIDTH, :])
    y = y + _dot(mem_ref[...].astype(BF16), w_ref[MIX_WIDTH:, :])
    y_ref[...] = x_ref[...] + y


def _out_proj(mix, mem_out, x, w_bf, *, layer, tm):
    n, d = x.shape
    return pl.pallas_call(
        _out_proj_kernel,
        grid=(n // tm,),
        in_specs=[
            pl.BlockSpec((tm, MIX_WIDTH), lambda i: (i, 0)),
            pl.BlockSpec((tm, MEM_WIDTH), lambda i: (i, 0)),
            pl.BlockSpec((tm, d), lambda i: (i, 0)),
            pl.BlockSpec((None, d, d), lambda i: (layer, 0, 0)),
        ],
        out_specs=pl.BlockSpec((tm, d), lambda i: (i, 0)),
        out_shape=jax.ShapeDtypeStruct((n, d), F32),
        compiler_params=_cparams(("arbitrary",)),
        name="out_proj",
    )(mix, mem_out, x, w_bf)


MEM_CHAINS = 2


def _mem_out_kernel(mix_ref, mq_ref, kt_ref, vt_ref, gain_ref, x_ref, w_ref, y_ref):
    tm = x_ref.shape[1]
    chunks = [slice(c * (tm // MEM_CHAINS), (c + 1) * (tm // MEM_CHAINS)) for c in range(MEM_CHAINS)]
    ys = [_dot(mix_ref[0, r].astype(BF16), w_ref[:MIX_WIDTH, :]) for r in chunks]
    mems = _mem_attention_chains([mq_ref[0, r] for r in chunks], kt_ref[...], vt_ref[...], gain_ref[...])
    for r, y, mem in zip(chunks, ys, mems):
        y_ref[0, r] = x_ref[0, r] + (y + _dot(mem.astype(BF16), w_ref[MIX_WIDTH:, :]))


def _mem_out(mix, mq, mkt, mvt, q_gain_full, x, w_bf, *, layer, tm):
    b, t, d = x.shape
    rows = lambda width: pl.BlockSpec((1, tm, width), lambda bi, i: (bi, i, 0))
    kv_spec = pl.BlockSpec((None, None, MEM_WIDTH, MEM_TOKENS), lambda bi, i: (layer, bi, 0, 0))
    return pl.pallas_call(
        _mem_out_kernel,
        grid=(b, t // tm),
        in_specs=[rows(MIX_WIDTH), rows(MEM_WIDTH), kv_spec, kv_spec,
                  pl.BlockSpec((1, MEM_WIDTH), lambda bi, i: (0, 0)), rows(d),
                  pl.BlockSpec((None, d, d), lambda bi, i: (layer, 0, 0))],
        out_specs=rows(d),
        out_shape=jax.ShapeDtypeStruct((b, t, d), F32),
        compiler_params=_cparams(("arbitrary", "arbitrary")),
        name="mem_out",
    )(mix, mq, mkt, mvt, q_gain_full.reshape(1, MEM_WIDTH), x, w_bf)


FF_CHUNK = 256
EXT_PAD = 8
UP_AHEAD = 11


def _conv_ffn_kernel(x_ref, gain_ref, wup_ref, cw_ref, cb_ref, wdown_ref, past_ref,
                     y_ref, state_ref, ext_ref, h_ref, acc_ref):
    t = pl.program_id(1)
    tm = x_ref.shape[1]
    lo = EXT_PAD - (CONV_W - 1)

    @pl.when(t == 0)
    def _():
        ext_ref[lo:EXT_PAD, :] = past_ref[0]

    @pl.when(t > 0)
    def _():
        ext_ref[lo:EXT_PAD, :] = ext_ref[tm + lo:tm + EXT_PAD, :]

    x = x_ref[0]
    h_ref[...] = _rms(x, gain_ref[...]).astype(BF16)
    acc_ref[...] = x
    n_chunks = D_FF // FF_CHUNK

    def chunk_cols(c):
        return [slice(half * D_FF + c * FF_CHUNK, half * D_FF + (c + 1) * FF_CHUNK) for half in range(2)]

    def up_project(c):
        for cols in chunk_cols(c):
            ext_ref[EXT_PAD:EXT_PAD + tm, cols] = _dot(h_ref[...], wup_ref[:, cols])

    for c in range(min(UP_AHEAD, n_chunks)):
        up_project(c)
    for c in range(n_chunks):
        if c + UP_AHEAD < n_chunks:
            up_project(c + UP_AHEAD)
        conv = []
        for cols in chunk_cols(c):
            y = cb_ref[:, cols]
            for j in range(CONV_W):
                y = y + cw_ref[j:j + 1, cols] * ext_ref[lo + j:lo + j + tm, cols]
            conv.append(y)
        g = (jax.nn.silu(conv[0]) * conv[1]).astype(BF16)
        acc_ref[...] += _dot(g, wdown_ref[c * FF_CHUNK:(c + 1) * FF_CHUNK, :])
    y_ref[0] = acc_ref[...]
    state_ref[0] = ext_ref[tm + lo:tm + EXT_PAD, :]


def _conv_ffn(x, gain, wup_bf, cw, cb, wdown_bf, past, *, layer, tm):
    b, t, d = x.shape
    ff2 = 2 * D_FF
    const = lambda bi, i: (0, 0)
    return pl.pallas_call(
        _conv_ffn_kernel,
        grid=(b, t // tm),
        in_specs=[
            pl.BlockSpec((1, tm, d), lambda bi, i: (bi, i, 0)),
            pl.BlockSpec((1, d), const),
            pl.BlockSpec((None, d, ff2), lambda bi, i: (layer, 0, 0), pipeline_mode=pl.Buffered(1)),
            pl.BlockSpec((CONV_W, ff2), const),
            pl.BlockSpec((1, ff2), const),
            pl.BlockSpec((None, D_FF, d), lambda bi, i: (layer, 0, 0), pipeline_mode=pl.Buffered(1)),
            pl.BlockSpec((1, CONV_W - 1, ff2), lambda bi, i: (bi, 0, 0)),
        ],
        out_specs=[
            pl.BlockSpec((1, tm, d), lambda bi, i: (bi, i, 0)),
            pl.BlockSpec((1, CONV_W - 1, ff2), lambda bi, i: (bi, 0, 0)),
        ],
        out_shape=[jax.ShapeDtypeStruct((b, t, d), F32), jax.ShapeDtypeStruct((b, CONV_W - 1, ff2), F32)],
        scratch_shapes=[pltpu.VMEM((EXT_PAD + tm, ff2), F32), pltpu.VMEM((tm, d), BF16), pltpu.VMEM((tm, d), F32)],
        compiler_params=_cparams(("arbitrary", "arbitrary")),
        name="conv_ffn",
    )(x, gain.reshape(1, d), wup_bf, cw, cb.reshape(1, ff2), wdown_bf, past)


def _conv_ffn_rows_kernel(x_ref, gain_ref, wup_ref, cw_ref, cb_ref, wdown_ref, prev1_ref, prev2_ref,
                          y_ref, up_ref, h_ref, acc_ref, *, seq_len):
    n = x_ref.shape[0]
    x = x_ref[...]
    h_ref[...] = _rms(x, gain_ref[...]).astype(BF16)
    acc_ref[...] = x
    t_idx = lax.broadcasted_iota(jnp.int32, (n, FF_CHUNK), 0) % seq_len
    n_chunks = D_FF // FF_CHUNK
    for c in range(n_chunks):
        for half in range(2):
            cols = slice(half * D_FF + c * FF_CHUNK, half * D_FF + (c + 1) * FF_CHUNK)
            up_ref[:, cols] = _dot(h_ref[...], wup_ref[:, cols])
    for c in range(n_chunks):
        conv = []
        for half in range(2):
            cols = slice(half * D_FF + c * FF_CHUNK, half * D_FF + (c + 1) * FF_CHUNK)
            up = up_ref[:, cols]
            back1 = jnp.where(t_idx >= 1, pltpu.roll(up, 1, 0), prev1_ref[:, cols])
            back2 = jnp.where(t_idx >= 2, pltpu.roll(up, 2, 0), prev2_ref[:, cols])
            conv.append(cb_ref[:, cols] + cw_ref[2:3, cols] * up + cw_ref[1:2, cols] * back1
                        + cw_ref[0:1, cols] * back2)
        g = (jax.nn.silu(conv[0]) * conv[1]).astype(BF16)
        acc_ref[...] += _dot(g, wdown_ref[c * FF_CHUNK:(c + 1) * FF_CHUNK, :])
    y_ref[...] = acc_ref[...]


def _conv_ffn_rows(x, gain, wup_bf, cw, cb, wdown_bf, past, *, layer, seq_len):
    n, d = x.shape
    ff2 = 2 * D_FF
    b = n // seq_len
    prev2 = jnp.pad(past, ((0, 0), (0, seq_len - 2), (0, 0))).reshape(n, ff2)
    prev1 = jnp.pad(past[:, 1:], ((0, 0), (0, seq_len - 1), (0, 0))).reshape(n, ff2)
    tm = FFN_SAMPLE_TILE
    const = lambda i: (0, 0)
    rows = lambda width: pl.BlockSpec((tm, width), lambda i: (i, 0))
    y, up = pl.pallas_call(
        functools.partial(_conv_ffn_rows_kernel, seq_len=seq_len),
        grid=(n // tm,),
        in_specs=[rows(d), pl.BlockSpec((1, d), const),
                  pl.BlockSpec((None, d, ff2), lambda i: (layer, 0, 0), pipeline_mode=pl.Buffered(1)),
                  pl.BlockSpec((CONV_W, ff2), const), pl.BlockSpec((1, ff2), const),
                  pl.BlockSpec((None, D_FF, d), lambda i: (layer, 0, 0), pipeline_mode=pl.Buffered(1)),
                  rows(ff2), rows(ff2)],
        out_specs=[rows(d), rows(ff2)],
        out_shape=[jax.ShapeDtypeStruct((n, d), F32), jax.ShapeDtypeStruct((n, ff2), F32)],
        scratch_shapes=[pltpu.VMEM((tm, d), BF16), pltpu.VMEM((tm, d), F32)],
        compiler_params=_cparams(("arbitrary",)),
        name="conv_ffn_rows",
    )(x, gain.reshape(1, d), wup_bf, cw, cb.reshape(1, ff2), wdown_bf, prev1, prev2)
    return y, up.reshape(b, seq_len, ff2)[:, seq_len - (CONV_W - 1):]


def _feature_major(x):
    lead = x.shape[:-3]
    n, h, dh = x.shape[-3:]
    nd = len(lead)
    return jnp.transpose(x, (*range(nd), nd + 1, nd + 2, nd)).reshape(*lead, h * dh, n)


def _position_major(xt, n_heads):
    lead = xt.shape[:-2]
    w, n = xt.shape[-2:]
    nd = len(lead)
    x4 = xt.reshape(*lead, n_heads, w // n_heads, n)
    return jnp.transpose(x4, (*range(nd), nd + 2, nd, nd + 1))


def kernel(x_prompt, x_sample, cache_mem_k, cache_mem_v, cache_sb_k, cache_sb_v, state_ffn_conv, page_table, mem_prompt, norm_mix, norm_ffn, norm_mem, w_in_a, gm_v_norm, gm_ws, gm_bs, w_in_b, sb_bias, w_mem_kv, mem_q_norm, mem_k_norm, w_out, w_up, conv_w, conv_b, w_down):
    depth = norm_mix.shape[0]
    bp, tp, d = x_prompt.shape
    bs_, ts, _ = x_sample.shape
    n_p, n_s = bp * tp, bs_ * ts

    k_gain_col = jnp.tile(mem_k_norm, (1, MEM_HEADS)).reshape(depth, MEM_WIDTH, 1)
    mem_kt_p, mem_vt_p = _mem_kv(mem_prompt, norm_mem, jnp.swapaxes(w_mem_kv, 1, 2).astype(BF16), k_gain_col)
    q_gain_full = jnp.tile(mem_q_norm, (1, MEM_HEADS))
    mem_kt_s = _feature_major(cache_mem_k)
    mem_vt_s = _feature_major(cache_mem_v)

    w_out_bf = w_out.astype(BF16)
    w_up_bf = w_up.astype(BF16)
    w_down_bf = w_down.astype(BF16)

    xp = x_prompt.reshape(n_p, d)
    xs = x_sample.reshape(n_s, d)
    sb_k_p, sb_v_p, sb_k_s, sb_v_s, gm_v_s, conv_p, conv_s = [], [], [], [], [], [], []
    for i in range(depth):
        if i % 2 == 0:
            a = i // 2
            w_bf = w_in_a[a].astype(BF16)
            bs_t = gm_bs[a].T
            mix_p, mq_p = _gmlp_in(xp, norm_mix[i], w_bf, gm_v_norm[a], gm_ws[a], bs_t,
                                   seq_len=CHUNK, cm=CHUNK, tm=PROJ_TILE, emit_v=False)
            mix_s, mq_s, v_new = _gmlp_in(xs, norm_mix[i], w_bf, gm_v_norm[a], gm_ws[a], bs_t,
                                          seq_len=ts, cm=n_s, tm=n_s, emit_v=True)
            gm_v_s.append(v_new.reshape(bs_, ts, MIX_WIDTH))
        else:
            b = i // 2
            w = w_in_b[b]
            wq_bf = jnp.concatenate([w[:, :MIX_WIDTH], w[:, 3 * MIX_WIDTH:]], axis=1).astype(BF16)
            wkvt_bf = w[:, MIX_WIDTH:3 * MIX_WIDTH].T.astype(BF16)
            q_p, kt_p, vt_p, ktb_p, vtb_p, mq_p = _sb_in_prompt(
                xp.reshape(bp, tp, d), norm_mix[i], wq_bf, wkvt_bf, tm=PROJ_TILE, tk=SB_TILE)
            mq_p = mq_p.reshape(n_p, MEM_WIDTH)
            q_s, k_s, v_s, mq_s = _sb_in_rows(xs, norm_mix[i], w.astype(BF16))
            mix_p = _sb_prompt(q_p, ktb_p, vtb_p, sb_bias[b], tq=SB_TILE).reshape(n_p, MIX_WIDTH)
            mix_s = _sb_decode(q_s.reshape(bs_, ts, MIX_WIDTH), k_s.reshape(bs_, ts, MIX_WIDTH),
                               v_s.reshape(bs_, ts, MIX_WIDTH),
                               _feature_major(cache_sb_k[b]), _feature_major(cache_sb_v[b]),
                               page_table, sb_bias[b]).reshape(n_s, MIX_WIDTH)
            sb_k_p.append(_position_major(kt_p, SB_HEADS))
            sb_v_p.append(_position_major(vt_p, SB_HEADS))
            sb_k_s.append(k_s.reshape(bs_, ts, SB_HEADS, SB_HEAD_DIM))
            sb_v_s.append(v_s.reshape(bs_, ts, SB_HEADS, SB_HEAD_DIM))
        xp3 = _mem_out(mix_p.reshape(bp, tp, MIX_WIDTH), mq_p.reshape(bp, tp, MEM_WIDTH), mem_kt_p, mem_vt_p,
                       q_gain_full[i], xp.reshape(bp, tp, d), w_out_bf, layer=i, tm=MEM_OUT_TILE)
        mem_out_s = _mem_attend(mq_s.reshape(bs_, ts, MEM_WIDTH), mem_kt_s, mem_vt_s, q_gain_full[i],
                                layer=i, seqs=MEM_SEQS_PER_STEP)
        xs = _out_proj(mix_s, mem_out_s.reshape(n_s, MEM_WIDTH), xs, w_out_bf, layer=i, tm=n_s)
        zero_rows = jnp.zeros((bp, CONV_W - 1, 2 * D_FF), F32)
        xp3, cp = _conv_ffn(xp3, norm_ffn[i], w_up_bf, conv_w[i], conv_b[i], w_down_bf,
                            zero_rows, layer=i, tm=FFN_TILE)
        xs, cs = _conv_ffn_rows(xs, norm_ffn[i], w_up_bf, conv_w[i], conv_b[i], w_down_bf,
                                state_ffn_conv[i], layer=i, seq_len=ts)
        xp = xp3.reshape(n_p, d)
        conv_p.append(cp)
        conv_s.append(cs)

    return (xp.reshape(bp, tp, d), xs.reshape(bs_, ts, d),
            _position_major(mem_kt_p, MEM_HEADS), _position_major(mem_vt_p, MEM_HEADS),
            jnp.stack(sb_k_p), jnp.stack(sb_v_p), jnp.stack(sb_k_s), jnp.stack(sb_v_s),
            jnp.stack(gm_v_s), jnp.stack(conv_p), jnp.stack(conv_s))
```

```python
import functools

import jax
import jax.numpy as jnp
from jax import lax
from jax.experimental import pallas as pl
from jax.experimental.pallas import tpu as pltpu

D_MODEL = 1024
MIX_WIDTH = 768
MEM_WIDTH = 256
CHUNK = 128
GM_GROUP_DIM = 128
GM_GROUPS = 6
SB_HEAD_DIM = 64
SB_HEADS = 12
MEM_TOKENS = 256
MEM_HEADS = 4
MEM_HEAD_DIM = 64
D_FF = 2816
CONV_W = 3
PAGE_SIZE = 128
EPS = 1e-6

VMEM_LIMIT = 56 * 1024 * 1024
PROJ_TILE = 512
MEM_OUT_TILE = 1024
FFN_TILE = 512
FFN_SAMPLE_TILE = 128
SB_TILE = 256
MEM_SEQS_PER_STEP = 8
F32 = jnp.float32
BF16 = jnp.bfloat16
LOG2E = 1.4426950408889634
SB_Q_SCALE = SB_HEAD_DIM ** -0.5 * LOG2E
SOFTPLUS_LINEAR_FROM = 64.0


def _cparams(sem):
    return pltpu.CompilerParams(dimension_semantics=sem, vmem_limit_bytes=VMEM_LIMIT)


def _rms(x, gain):
    ms = jnp.mean(x * x, axis=-1, keepdims=True)
    return x * lax.rsqrt(ms + EPS) * gain


def _nt_dot(a, b):
    return lax.dot_general(a, b, (((1,), (1,)), ((), ())), preferred_element_type=F32)


def _dot(a, b):
    return jnp.dot(a, b, preferred_element_type=F32)


def _split(x):
    hi = x.astype(BF16)
    return hi, (x - hi.astype(F32)).astype(BF16)


def _same_head(width, head_dim):
    r = lax.broadcasted_iota(jnp.int32, (width, width), 0) // head_dim
    c = lax.broadcasted_iota(jnp.int32, (width, width), 1) // head_dim
    return (r == c).astype(BF16)


def _head_stack(q, n_heads, head_dim):
    q = q.astype(F32)
    head = lax.broadcasted_iota(jnp.int32, q.shape, 1) // head_dim
    return jnp.concatenate([jnp.where(head == h, q, 0.0) for h in range(n_heads)], axis=0).astype(BF16)


def _head_unstack(o, n_heads, head_dim):
    t = o.shape[0] // n_heads
    head = lax.broadcasted_iota(jnp.int32, (t, o.shape[1]), 1) // head_dim
    out = jnp.zeros((t, o.shape[1]), o.dtype)
    for h in range(n_heads):
        out = out + jnp.where(head == h, o[h * t:(h + 1) * t], 0.0)
    return out


def _mem_kv_kernel(mem_ref, gain_ref, wt_ref, kgain_ref, k_ref, v_ref):
    h = _rms(mem_ref[0], gain_ref[0]).astype(BF16)
    mt = _nt_dot(wt_ref[0], h)
    kt = mt[:MEM_WIDTH]
    hi, lo = _split(kt * kt)
    same = _same_head(MEM_WIDTH, MEM_HEAD_DIM)
    ss = _dot(same, hi) + _dot(same, lo)
    k_ref[0, 0] = kt * lax.rsqrt(ss * (1.0 / MEM_HEAD_DIM) + EPS) * kgain_ref[0]
    v_ref[0, 0] = mt[MEM_WIDTH:]


def _mem_kv(mem, norm_mem, w_kv_t_bf, k_gain_col):
    depth = norm_mem.shape[0]
    batch, m_tok, d = mem.shape
    out = jax.ShapeDtypeStruct((depth, batch, MEM_WIDTH, m_tok), F32)
    return pl.pallas_call(
        _mem_kv_kernel,
        grid=(depth, batch),
        in_specs=[
            pl.BlockSpec((1, m_tok, d), lambda i, b: (b, 0, 0)),
            pl.BlockSpec((1, 1, d), lambda i, b: (i, 0, 0)),
            pl.BlockSpec((1, 2 * MEM_WIDTH, d), lambda i, b: (i, 0, 0)),
            pl.BlockSpec((1, MEM_WIDTH, 1), lambda i, b: (i, 0, 0)),
        ],
        out_specs=[
            pl.BlockSpec((1, 1, MEM_WIDTH, m_tok), lambda i, b: (i, b, 0, 0)),
            pl.BlockSpec((1, 1, MEM_WIDTH, m_tok), lambda i, b: (i, b, 0, 0)),
        ],
        out_shape=[out, out],
        compiler_params=_cparams(("arbitrary", "arbitrary")),
        name="mem_kv",
    )(mem, norm_mem.reshape(depth, 1, d), w_kv_t_bf, k_gain_col)


def _gmlp_in_kernel(x_ref, gain_ref, w_ref, vgain_ref, ws_ref, bst_ref, mix_ref, mq_ref, v_ref, *, seq_len, cm):
    tm = x_ref.shape[0]
    h = _rms(x_ref[...], gain_ref[...]).astype(BF16)
    proj = _dot(h, w_ref[...])
    mq_ref[...] = proj[:, 2 * MIX_WIDTH:]
    v_ref[...] = _rms(jax.nn.gelu(proj[:, MIX_WIDTH:2 * MIX_WIDTH]), vgain_ref[...])
    mix_ref[...] = jax.nn.gelu(proj[:, :MIX_WIDTH])
    r = lax.broadcasted_iota(jnp.int32, (cm, cm), 0)
    c = lax.broadcasted_iota(jnp.int32, (cm, cm), 1)
    allowed = (r // seq_len == c // seq_len) & (c % seq_len <= r % seq_len)
    if seq_len != cm:
        pick = (lax.broadcasted_iota(jnp.int32, (cm, CHUNK), 1)
                == lax.broadcasted_iota(jnp.int32, (cm, CHUNK), 0) % seq_len).astype(BF16)
    for g in range(GM_GROUPS):
        cols = slice(g * GM_GROUP_DIM, (g + 1) * GM_GROUP_DIM)
        wg = ws_ref[g].astype(BF16)
        if seq_len != cm:
            wg = _nt_dot(_dot(pick, wg).astype(BF16), pick)
        wg = jnp.where(allowed, wg, 0.0).astype(BF16)
        bias_col = jnp.concatenate([bst_ref[:seq_len, g:g + 1]] * (cm // seq_len), axis=0)
        for ci in range(tm // cm):
            rows = slice(ci * cm, (ci + 1) * cm)
            mixed = _dot(wg, v_ref[rows, cols].astype(BF16)) + bias_col
            mix_ref[rows, cols] = mix_ref[rows, cols] * mixed


def _gmlp_in(x, gain, w_bf, v_gain, ws, bs_t, *, seq_len, cm, tm, emit_v):
    n, d = x.shape
    d_in = w_bf.shape[1]
    const = lambda i: (0, 0)
    out_shape = [jax.ShapeDtypeStruct((n, MIX_WIDTH), F32), jax.ShapeDtypeStruct((n, MEM_WIDTH), F32)]
    out_specs = [pl.BlockSpec((tm, MIX_WIDTH), lambda i: (i, 0)), pl.BlockSpec((tm, MEM_WIDTH), lambda i: (i, 0))]
    scratch = []
    if emit_v:
        out_shape.append(jax.ShapeDtypeStruct((n, MIX_WIDTH), F32))
        out_specs.append(pl.BlockSpec((tm, MIX_WIDTH), lambda i: (i, 0)))
    else:
        scratch.append(pltpu.VMEM((tm, MIX_WIDTH), F32))
    return pl.pallas_call(
        functools.partial(_gmlp_in_kernel, seq_len=seq_len, cm=cm),
        grid=(n // tm,),
        in_specs=[
            pl.BlockSpec((tm, d), lambda i: (i, 0)),
            pl.BlockSpec((1, d), const),
            pl.BlockSpec((d, d_in), const),
            pl.BlockSpec((1, MIX_WIDTH), const),
            pl.BlockSpec((GM_GROUPS, CHUNK, CHUNK), lambda i: (0, 0, 0)),
            pl.BlockSpec((CHUNK, GM_GROUPS), const),
        ],
        out_specs=out_specs,
        out_shape=out_shape,
        scratch_shapes=scratch,
        compiler_params=_cparams(("arbitrary",)),
        name="gmlp_in",
    )(x, gain.reshape(1, d), w_bf, v_gain.reshape(1, MIX_WIDTH), ws, bs_t)


def _sb_in_rows_kernel(x_ref, gain_ref, w_ref, q_ref, k_ref, v_ref, mq_ref):
    h = _rms(x_ref[...], gain_ref[...]).astype(BF16)
    proj = _dot(h, w_ref[...])
    q_ref[...] = (proj[:, :MIX_WIDTH] * SB_Q_SCALE).astype(BF16)
    k_ref[...] = proj[:, MIX_WIDTH:2 * MIX_WIDTH]
    v_ref[...] = proj[:, 2 * MIX_WIDTH:3 * MIX_WIDTH]
    mq_ref[...] = proj[:, 3 * MIX_WIDTH:]


def _sb_in_rows(x, gain, w_bf):
    n, d = x.shape
    full = lambda width: pl.BlockSpec((n, width), lambda i: (0, 0))
    shp = lambda width, dt: jax.ShapeDtypeStruct((n, width), dt)
    return pl.pallas_call(
        _sb_in_rows_kernel,
        grid=(1,),
        in_specs=[full(d), pl.BlockSpec((1, d), lambda i: (0, 0)), pl.BlockSpec(w_bf.shape, lambda i: (0, 0))],
        out_specs=[full(MIX_WIDTH)] * 3 + [full(MEM_WIDTH)],
        out_shape=[shp(MIX_WIDTH, BF16), shp(MIX_WIDTH, F32), shp(MIX_WIDTH, F32), shp(MEM_WIDTH, F32)],
        compiler_params=_cparams(("arbitrary",)),
        name="sb_in_rows",
    )(x, gain.reshape(1, d), w_bf)


def _sb_in_prompt_kernel(x_ref, gain_ref, wq_ref, wkvt_ref, q_ref, kt_ref, vt_ref, ktb_ref, vtb_ref, mq_ref, *, tk):
    tm = x_ref.shape[1]
    h = _rms(x_ref[0], gain_ref[...]).astype(BF16)
    qm = _dot(h, wq_ref[...])
    q_ref[0] = (qm[:, :MIX_WIDTH] * SB_Q_SCALE).astype(BF16)
    mq_ref[0] = qm[:, MIX_WIDTH:]
    kvt = _nt_dot(wkvt_ref[...], h)
    kt_ref[0] = kvt[:MIX_WIDTH]
    vt_ref[0] = kvt[MIX_WIDTH:]
    for c in range(tm // tk):
        ktb_ref[0, c] = kvt[:MIX_WIDTH, c * tk:(c + 1) * tk].astype(BF16)
        vtb_ref[0, c] = kvt[MIX_WIDTH:, c * tk:(c + 1) * tk].astype(BF16)


def _sb_in_prompt(x, gain, wq_bf, wkvt_bf, *, tm, tk):
    b, t, d = x.shape
    const = lambda bi, i: (0, 0)
    return pl.pallas_call(
        functools.partial(_sb_in_prompt_kernel, tk=tk),
        grid=(b, t // tm),
        in_specs=[
            pl.BlockSpec((1, tm, d), lambda bi, i: (bi, i, 0)),
            pl.BlockSpec((1, d), const),
            pl.BlockSpec((d, MIX_WIDTH + MEM_WIDTH), const),
            pl.BlockSpec((2 * MIX_WIDTH, d), const),
        ],
        out_specs=[
            pl.BlockSpec((1, tm, MIX_WIDTH), lambda bi, i: (bi, i, 0)),
            pl.BlockSpec((1, MIX_WIDTH, tm), lambda bi, i: (bi, 0, i)),
            pl.BlockSpec((1, MIX_WIDTH, tm), lambda bi, i: (bi, 0, i)),
            pl.BlockSpec((1, tm // tk, MIX_WIDTH, tk), lambda bi, i: (bi, i, 0, 0)),
            pl.BlockSpec((1, tm // tk, MIX_WIDTH, tk), lambda bi, i: (bi, i, 0, 0)),
            pl.BlockSpec((1, tm, MEM_WIDTH), lambda bi, i: (bi, i, 0)),
        ],
        out_shape=[
            jax.ShapeDtypeStruct((b, t, MIX_WIDTH), BF16),
            jax.ShapeDtypeStruct((b, MIX_WIDTH, t), F32),
            jax.ShapeDtypeStruct((b, MIX_WIDTH, t), F32),
            jax.ShapeDtypeStruct((b, t // tk, MIX_WIDTH, tk), BF16),
            jax.ShapeDtypeStruct((b, t // tk, MIX_WIDTH, tk), BF16),
            jax.ShapeDtypeStruct((b, t, MEM_WIDTH), F32),
        ],
        compiler_params=_cparams(("arbitrary", "arbitrary")),
        name="sb_in_prompt",
    )(x, gain.reshape(1, d), wq_bf, wkvt_bf)


def _sb_block(q_stack, k, v, bias_col, neg_tri, carry, mask, feature_major, n_chains=1):
    tk = neg_tri.shape[0]
    n_rows = q_stack.shape[0]
    chains = [slice(c * (n_rows // n_chains), (c + 1) * (n_rows // n_chains)) for c in range(n_chains)]
    masks = [None if mask is None else mask[r] for r in chains]
    ws = [(_dot(q_stack[r], k) if feature_major else _nt_dot(q_stack[r], k)) + bias_col[r] for r in chains]
    sps = []
    for w, m in zip(ws, masks):
        sp = jnp.where(w > SOFTPLUS_LINEAR_FROM, w, jnp.log(1.0 + jnp.exp2(w)) * LOG2E)
        if m is not None:
            sp = jnp.where(m, sp, 0.0)
        sps.append(sp.astype(BF16))
    later_first = list(reversed(range(sps[0].shape[1] // tk)))
    sums = [[] for _ in sps]
    for g in range(0, len(later_first), 2):
        for chain_sums, sp_bf in zip(sums, sps):
            for j in later_first[g:g + 2]:
                chain_sums.append(_dot(sp_bf[:, j * tk:(j + 1) * tk], neg_tri))
    contribs, carries = [], []
    for w, m, r, from_here in zip(ws, masks, chains, sums):
        c = carry[r]
        parts = []
        for i, fh in enumerate(from_here):
            j = len(from_here) - 1 - i
            parts.append(w[:, j * tk:(j + 1) * tk] + (fh + c))
            c = c + fh[:, 0:1]
        a = jnp.exp2(parts[0] if len(parts) == 1 else jnp.concatenate(parts[::-1], axis=1))
        if m is not None:
            a = jnp.where(m, a, 0.0)
        a = a.astype(BF16)
        contribs.append(_nt_dot(a, v) if feature_major else _dot(a, v))
        carries.append(c)
    if n_chains == 1:
        return contribs[0], carries[0]
    return jnp.concatenate(contribs, axis=0), jnp.concatenate(carries, axis=0)


def _neg_tri(tk):
    j = lax.broadcasted_iota(jnp.int32, (tk, tk), 0)
    s = lax.broadcasted_iota(jnp.int32, (tk, tk), 1)
    return jnp.where(j >= s, -1.0, 0.0).astype(BF16)


HEADS_PER_GROUP = 4
GROUP_WIDTH = HEADS_PER_GROUP * SB_HEAD_DIM
SB_RUN = 4
SB_CHAINS = 2


def _sb_prompt_kernel(bias_ref, q_ref, k_ref, v_ref, o_ref, qs_ref, acc_ref, carry_ref, *, tq):
    hg = pl.program_id(1)
    bias_col = jnp.concatenate(
        [jnp.full((tq, 1), bias_ref[hg * HEADS_PER_GROUP + h] * LOG2E, F32) for h in range(HEADS_PER_GROUP)], axis=0)
    tri = _neg_tri(tq)

    def query_tile(qi, _):
        q_rows = pl.ds(pl.multiple_of(qi * tq, tq), tq)
        qs_ref[...] = _head_stack(q_ref[0, q_rows, :], HEADS_PER_GROUP, SB_HEAD_DIM)
        _sb_prompt_tile(qi, k_ref, v_ref, qs_ref, acc_ref, carry_ref, bias_col, tri, tq)
        o_ref[0, q_rows, :] = _head_unstack(acc_ref[...], HEADS_PER_GROUP, SB_HEAD_DIM)
        return 0

    lax.fori_loop(0, q_ref.shape[1] // tq, query_tile, 0)


def _sb_prompt_tile(qi, k_ref, v_ref, qs_ref, acc_ref, carry_ref, bias_col, tri, tq):
    rows = HEADS_PER_GROUP * tq

    def key_run(kj, n_blocks, diagonal):
        k = jnp.concatenate([k_ref[0, kj + i] for i in range(n_blocks)], axis=1)
        v = jnp.concatenate([v_ref[0, kj + i] for i in range(n_blocks)], axis=1)
        if diagonal:
            t_row = lax.broadcasted_iota(jnp.int32, (rows, tq), 0) % tq
            mask = lax.broadcasted_iota(jnp.int32, (rows, tq), 1) < t_row
            carry = jnp.zeros((rows, 1), F32)
        else:
            mask = None
            carry = carry_ref[...]
        contrib, carry = _sb_block(qs_ref[...], k, v, bias_col, tri, carry, mask, True, n_chains=SB_CHAINS)
        if diagonal:
            acc_ref[...] = contrib
        else:
            acc_ref[...] += contrib
        carry_ref[...] = carry

    key_run(qi, 1, True)
    n_runs = qi // SB_RUN
    top = qi
    size = 1
    while size < SB_RUN:
        has_run = (qi // size) % 2 == 1

        @pl.when(has_run)
        def _(top=top, size=size):
            key_run(top - size, size, False)

        top = top - jnp.where(has_run, size, 0)
        size *= 2

    def run(i, _):
        key_run(SB_RUN * (n_runs - 1 - i), SB_RUN, False)
        return 0

    lax.fori_loop(0, n_runs, run, 0)


def _sb_prompt(q_bf, kt_bf, vt_bf, bias, *, tq):
    b, t, w = q_bf.shape
    nkb = t // tq
    return pl.pallas_call(
        functools.partial(_sb_prompt_kernel, tq=tq),
        grid=(b, w // GROUP_WIDTH),
        in_specs=[
            pl.BlockSpec(memory_space=pltpu.SMEM),
            pl.BlockSpec((1, t, GROUP_WIDTH), lambda bi, g: (bi, 0, g)),
            pl.BlockSpec((1, nkb, GROUP_WIDTH, tq), lambda bi, g: (bi, 0, g, 0)),
            pl.BlockSpec((1, nkb, GROUP_WIDTH, tq), lambda bi, g: (bi, 0, g, 0)),
        ],
        out_specs=pl.BlockSpec((1, t, GROUP_WIDTH), lambda bi, g: (bi, 0, g)),
        out_shape=jax.ShapeDtypeStruct((b, t, w), F32),
        scratch_shapes=[pltpu.VMEM((HEADS_PER_GROUP * tq, GROUP_WIDTH), BF16),
                        pltpu.VMEM((HEADS_PER_GROUP * tq, GROUP_WIDTH), F32),
                        pltpu.VMEM((HEADS_PER_GROUP * tq, 1), F32)],
        compiler_params=_cparams(("arbitrary", "arbitrary")),
        name="sb_prompt",
    )(bias, q_bf, kt_bf, vt_bf)


PAGES_PER_STEP = 16


def _sb_decode_kernel(pt_ref, bias_ref, q_ref, kn_ref, vn_ref, *refs, t_new):
    k_pages = refs[:PAGES_PER_STEP]
    v_pages = refs[PAGES_PER_STEP:2 * PAGES_PER_STEP]
    o_ref, acc_ref, carry_ref = refs[2 * PAGES_PER_STEP:]
    j = pl.program_id(1)
    rows = SB_HEADS * t_new
    q_stack = _head_stack(q_ref[0], SB_HEADS, SB_HEAD_DIM)
    bias_col = jnp.concatenate([jnp.full((t_new, 1), bias_ref[h] * LOG2E, F32) for h in range(SB_HEADS)], axis=0)

    @pl.when(j == 0)
    def _():
        pad = jnp.zeros((PAGE_SIZE - t_new, MIX_WIDTH), F32)
        k = jnp.concatenate([kn_ref[0], pad], axis=0).astype(BF16)
        v = jnp.concatenate([vn_ref[0], pad], axis=0).astype(BF16)
        t_row = lax.broadcasted_iota(jnp.int32, (rows, PAGE_SIZE), 0) % t_new
        s_col = lax.broadcasted_iota(jnp.int32, (rows, PAGE_SIZE), 1)
        contrib, carry = _sb_block(q_stack, k, v, bias_col, _neg_tri(PAGE_SIZE),
                                   jnp.zeros((rows, 1), F32), s_col < t_row, False)
        acc_ref[...] = contrib
        carry_ref[...] = carry

    k = jnp.concatenate([r[0].astype(BF16) for r in k_pages], axis=1)
    v = jnp.concatenate([r[0].astype(BF16) for r in v_pages], axis=1)
    contrib, carry = _sb_block(q_stack, k, v, bias_col, _neg_tri(2 * PAGE_SIZE), carry_ref[...], None, True)
    acc_ref[...] += contrib
    carry_ref[...] = carry

    @pl.when(j == pl.num_programs(1) - 1)
    def _():
        o_ref[0] = _head_unstack(acc_ref[...], SB_HEADS, SB_HEAD_DIM)


def _sb_decode(q_bf, k_new, v_new, cache_kt, cache_vt, page_table, bias):
    b, t_new, w = q_bf.shape
    n_pages = page_table.shape[1]
    n_steps = n_pages // PAGES_PER_STEP

    def page_spec(slot):
        def index_map(bi, j, pt, bias):
            return (pt[bi, (n_steps - 1 - j) * PAGES_PER_STEP + slot], 0, 0)
        return pl.BlockSpec((1, w, PAGE_SIZE), index_map)

    new_spec = pl.BlockSpec((1, t_new, w), lambda bi, j, pt, bias: (bi, 0, 0))
    grid_spec = pltpu.PrefetchScalarGridSpec(
        num_scalar_prefetch=2,
        grid=(b, n_steps),
        in_specs=[new_spec, new_spec, new_spec]
        + [page_spec(s) for s in range(PAGES_PER_STEP)] * 2,
        out_specs=new_spec,
        scratch_shapes=[pltpu.VMEM((SB_HEADS * t_new, w), F32), pltpu.VMEM((SB_HEADS * t_new, 1), F32)],
    )
    return pl.pallas_call(
        functools.partial(_sb_decode_kernel, t_new=t_new),
        grid_spec=grid_spec,
        out_shape=jax.ShapeDtypeStruct((b, t_new, w), F32),
        compiler_params=_cparams(("arbitrary", "arbitrary")),
        name="sb_decode",
    )(page_table, bias, q_bf, k_new, v_new, *([cache_kt] * PAGES_PER_STEP), *([cache_vt] * PAGES_PER_STEP))


def _mem_attention_chains(qs, kt, vt, gain):
    same = _same_head(MEM_WIDTH, MEM_HEAD_DIM)
    kt = kt.astype(BF16)
    vt = vt.astype(BF16)
    sq = []
    for q in qs:
        hi, lo = _split(q * q)
        sq.append(_dot(hi, same) + _dot(lo, same))
    scores = []
    for q, ss in zip(qs, sq):
        qn = q * lax.rsqrt(ss * (1.0 / MEM_HEAD_DIM) + EPS) * gain
        q_stack = _head_stack(qn * (MEM_HEAD_DIM ** -0.5), MEM_HEADS, MEM_HEAD_DIM)
        scores.append(_dot(q_stack, kt))
    outs = []
    for s in scores:
        p = jnp.exp(s - jnp.max(s, axis=-1, keepdims=True))
        l = jnp.sum(p, axis=-1, keepdims=True)
        outs.append((_nt_dot(p.astype(BF16), vt), l))
    return [_head_unstack(o / l, MEM_HEADS, MEM_HEAD_DIM) for o, l in outs]


def _mem_attention(q, kt, vt, gain):
    return _mem_attention_chains([q], kt, vt, gain)[0]


def _mem_attend_kernel(q_ref, kt_ref, vt_ref, gain_ref, o_ref):
    for s in range(q_ref.shape[0]):
        o_ref[s] = _mem_attention(q_ref[s], kt_ref[s], vt_ref[s], gain_ref[...])


def _mem_attend(q, mkt, mvt, q_gain_full, *, layer, seqs):
    b, t, w = q.shape
    kv_spec = pl.BlockSpec((None, seqs, w, MEM_TOKENS), lambda bi: (layer, bi, 0, 0))
    return pl.pallas_call(
        _mem_attend_kernel,
        grid=(b // seqs,),
        in_specs=[
            pl.BlockSpec((seqs, t, w), lambda bi: (bi, 0, 0)),
            kv_spec,
            kv_spec,
            pl.BlockSpec((1, w), lambda bi: (0, 0)),
        ],
        out_specs=pl.BlockSpec((seqs, t, w), lambda bi: (bi, 0, 0)),
        out_shape=jax.ShapeDtypeStruct((b, t, w), F32),
        compiler_params=_cparams(("arbitrary",)),
        name="mem_attend",
    )(q, mkt, mvt, q_gain_full.reshape(1, w))


def _out_proj_kernel(mix_ref, mem_ref, x_ref, w_ref, y_ref):
    y = _dot(mix_ref[...].astype(BF16), w_ref[:MIX_WIDTH, :])
    y = y + _dot(mem_ref[...].astype(BF16), w_ref[MIX_WIDTH:, :])
    y_ref[...] = x_ref[...] + y


def _out_proj(mix, mem_out, x, w_bf, *, layer, tm):
    n, d = x.shape
    return pl.pallas_call(
        _out_proj_kernel,
        grid=(n // tm,),
        in_specs=[
            pl.BlockSpec((tm, MIX_WIDTH), lambda i: (i, 0)),
            pl.BlockSpec((tm, MEM_WIDTH), lambda i: (i, 0)),
            pl.BlockSpec((tm, d), lambda i: (i, 0)),
            pl.BlockSpec((None, d, d), lambda i: (layer, 0, 0)),
        ],
        out_specs=pl.BlockSpec((tm, d), lambda i: (i, 0)),
        out_shape=jax.ShapeDtypeStruct((n, d), F32),
        compiler_params=_cparams(("arbitrary",)),
        name="out_proj",
    )(mix, mem_out, x, w_bf)


MEM_CHAINS = 4


def _mem_out_kernel(mix_ref, mq_ref, kt_ref, vt_ref, gain_ref, x_ref, w_ref, y_ref):
    tm = x_ref.shape[1]
    chunks = [slice(c * (tm // MEM_CHAINS), (c + 1) * (tm // MEM_CHAINS)) for c in range(MEM_CHAINS)]
    ys = [_dot(mix_ref[0, r].astype(BF16), w_ref[:MIX_WIDTH, :]) for r in chunks]
    mems = _mem_attention_chains([mq_ref[0, r] for r in chunks], kt_ref[...], vt_ref[...], gain_ref[...])
    for r, y, mem in zip(chunks, ys, mems):
        y_ref[0, r] = x_ref[0, r] + (y + _dot(mem.astype(BF16), w_ref[MIX_WIDTH:, :]))


def _mem_out(mix, mq, mkt, mvt, q_gain_full, x, w_bf, *, layer, tm):
    b, t, d = x.shape
    rows = lambda width: pl.BlockSpec((1, tm, width), lambda bi, i: (bi, i, 0))
    kv_spec = pl.BlockSpec((None, None, MEM_WIDTH, MEM_TOKENS), lambda bi, i: (layer, bi, 0, 0))
    return pl.pallas_call(
        _mem_out_kernel,
        grid=(b, t // tm),
        in_specs=[rows(MIX_WIDTH), rows(MEM_WIDTH), kv_spec, kv_spec,
                  pl.BlockSpec((1, MEM_WIDTH), lambda bi, i: (0, 0)), rows(d),
                  pl.BlockSpec((None, d, d), lambda bi, i: (layer, 0, 0))],
        out_specs=rows(d),
        out_shape=jax.ShapeDtypeStruct((b, t, d), F32),
        compiler_params=_cparams(("arbitrary", "arbitrary")),
        name="mem_out",
    )(mix, mq, mkt, mvt, q_gain_full.reshape(1, MEM_WIDTH), x, w_bf)


FF_CHUNK = 256
EXT_PAD = 8
UP_AHEAD = 11


def _conv_ffn_kernel(x_ref, gain_ref, wup_ref, cw_ref, cb_ref, wdown_ref, past_ref,
                     y_ref, state_ref, ext_ref, h_ref, acc_ref):
    t = pl.program_id(1)
    tm = x_ref.shape[1]
    lo = EXT_PAD - (CONV_W - 1)

    @pl.when(t == 0)
    def _():
        ext_ref[lo:EXT_PAD, :] = past_ref[0]

    @pl.when(t > 0)
    def _():
        ext_ref[lo:EXT_PAD, :] = ext_ref[tm + lo:tm + EXT_PAD, :]

    x = x_ref[0]
    h_ref[...] = _rms(x, gain_ref[...]).astype(BF16)
    acc_ref[...] = x
    n_chunks = D_FF // FF_CHUNK

    def chunk_cols(c):
        return [slice(half * D_FF + c * FF_CHUNK, half * D_FF + (c + 1) * FF_CHUNK) for half in range(2)]

    def up_project(c):
        for cols in chunk_cols(c):
            ext_ref[EXT_PAD:EXT_PAD + tm, cols] = _dot(h_ref[...], wup_ref[:, cols])

    for c in range(min(UP_AHEAD, n_chunks)):
        up_project(c)
    for c in range(n_chunks):
        if c + UP_AHEAD < n_chunks:
            up_project(c + UP_AHEAD)
        conv = []
        for cols in chunk_cols(c):
            y = cb_ref[:, cols]
            for j in range(CONV_W):
                y = y + cw_ref[j:j + 1, cols] * ext_ref[lo + j:lo + j + tm, cols]
            conv.append(y)
        g = (jax.nn.silu(conv[0]) * conv[1]).astype(BF16)
        acc_ref[...] += _dot(g, wdown_ref[c * FF_CHUNK:(c + 1) * FF_CHUNK, :])
    y_ref[0] = acc_ref[...]
    state_ref[0] = ext_ref[tm + lo:tm + EXT_PAD, :]


def _conv_ffn(x, gain, wup_bf, cw, cb, wdown_bf, past, *, layer, tm):
    b, t, d = x.shape
    ff2 = 2 * D_FF
    const = lambda bi, i: (0, 0)
    return pl.pallas_call(
        _conv_ffn_kernel,
        grid=(b, t // tm),
        in_specs=[
            pl.BlockSpec((1, tm, d), lambda bi, i: (bi, i, 0)),
            pl.BlockSpec((1, d), const),
            pl.BlockSpec((None, d, ff2), lambda bi, i: (layer, 0, 0), pipeline_mode=pl.Buffered(1)),
            pl.BlockSpec((CONV_W, ff2), const),
            pl.BlockSpec((1, ff2), const),
            pl.BlockSpec((None, D_FF, d), lambda bi, i: (layer, 0, 0), pipeline_mode=pl.Buffered(1)),
            pl.BlockSpec((1, CONV_W - 1, ff2), lambda bi, i: (bi, 0, 0)),
        ],
        out_specs=[
            pl.BlockSpec((1, tm, d), lambda bi, i: (bi, i, 0)),
            pl.BlockSpec((1, CONV_W - 1, ff2), lambda bi, i: (bi, 0, 0)),
        ],
        out_shape=[jax.ShapeDtypeStruct((b, t, d), F32), jax.ShapeDtypeStruct((b, CONV_W - 1, ff2), F32)],
        scratch_shapes=[pltpu.VMEM((EXT_PAD + tm, ff2), F32), pltpu.VMEM((tm, d), BF16), pltpu.VMEM((tm, d), F32)],
        compiler_params=_cparams(("arbitrary", "arbitrary")),
        name="conv_ffn",
    )(x, gain.reshape(1, d), wup_bf, cw, cb.reshape(1, ff2), wdown_bf, past)


def _conv_ffn_rows_kernel(x_ref, gain_ref, wup_ref, cw_ref, cb_ref, wdown_ref, prev1_ref, prev2_ref,
                          y_ref, up_ref, h_ref, acc_ref, *, seq_len):
    n = x_ref.shape[0]
    x = x_ref[...]
    h_ref[...] = _rms(x, gain_ref[...]).astype(BF16)
    acc_ref[...] = x
    t_idx = lax.broadcasted_iota(jnp.int32, (n, FF_CHUNK), 0) % seq_len
    n_chunks = D_FF // FF_CHUNK
    for c in range(n_chunks):
        for half in range(2):
            cols = slice(half * D_FF + c * FF_CHUNK, half * D_FF + (c + 1) * FF_CHUNK)
            up_ref[:, cols] = _dot(h_ref[...], wup_ref[:, cols])
    for c in range(n_chunks):
        conv = []
        for half in range(2):
            cols = slice(half * D_FF + c * FF_CHUNK, half * D_FF + (c + 1) * FF_CHUNK)
            up = up_ref[:, cols]
            back1 = jnp.where(t_idx >= 1, pltpu.roll(up, 1, 0), prev1_ref[:, cols])
            back2 = jnp.where(t_idx >= 2, pltpu.roll(up, 2, 0), prev2_ref[:, cols])
            conv.append(cb_ref[:, cols] + cw_ref[2:3, cols] * up + cw_ref[1:2, cols] * back1
                        + cw_ref[0:1, cols] * back2)
        g = (jax.nn.silu(conv[0]) * conv[1]).astype(BF16)
        acc_ref[...] += _dot(g, wdown_ref[c * FF_CHUNK:(c + 1) * FF_CHUNK, :])
    y_ref[...] = acc_ref[...]


def _conv_ffn_rows(x, gain, wup_bf, cw, cb, wdown_bf, past, *, layer, seq_len):
    n, d = x.shape
    ff2 = 2 * D_FF
    b = n // seq_len
    prev2 = jnp.pad(past, ((0, 0), (0, seq_len - 2), (0, 0))).reshape(n, ff2)
    prev1 = jnp.pad(past[:, 1:], ((0, 0), (0, seq_len - 1), (0, 0))).reshape(n, ff2)
    tm = FFN_SAMPLE_TILE
    const = lambda i: (0, 0)
    rows = lambda width: pl.BlockSpec((tm, width), lambda i: (i, 0))
    y, up = pl.pallas_call(
        functools.partial(_conv_ffn_rows_kernel, seq_len=seq_len),
        grid=(n // tm,),
        in_specs=[rows(d), pl.BlockSpec((1, d), const),
                  pl.BlockSpec((None, d, ff2), lambda i: (layer, 0, 0), pipeline_mode=pl.Buffered(1)),
                  pl.BlockSpec((CONV_W, ff2), const), pl.BlockSpec((1, ff2), const),
                  pl.BlockSpec((None, D_FF, d), lambda i: (layer, 0, 0), pipeline_mode=pl.Buffered(1)),
                  rows(ff2), rows(ff2)],
        out_specs=[rows(d), rows(ff2)],
        out_shape=[jax.ShapeDtypeStruct((n, d), F32), jax.ShapeDtypeStruct((n, ff2), F32)],
        scratch_shapes=[pltpu.VMEM((tm, d), BF16), pltpu.VMEM((tm, d), F32)],
        compiler_params=_cparams(("arbitrary",)),
        name="conv_ffn_rows",
    )(x, gain.reshape(1, d), wup_bf, cw, cb.reshape(1, ff2), wdown_bf, prev1, prev2)
    return y, up.reshape(b, seq_len, ff2)[:, seq_len - (CONV_W - 1):]


def _feature_major(x):
    lead = x.shape[:-3]
    n, h, dh = x.shape[-3:]
    nd = len(lead)
    return jnp.transpose(x, (*range(nd), nd + 1, nd + 2, nd)).reshape(*lead, h * dh, n)


def _position_major(xt, n_heads):
    lead = xt.shape[:-2]
    w, n = xt.shape[-2:]
    nd = len(lead)
    x4 = xt.reshape(*lead, n_heads, w // n_heads, n)
    return jnp.transpose(x4, (*range(nd), nd + 2, nd, nd + 1))


def kernel(x_prompt, x_sample, cache_mem_k, cache_mem_v, cache_sb_k, cache_sb_v, state_ffn_conv, page_table, mem_prompt, norm_mix, norm_ffn, norm_mem, w_in_a, gm_v_norm, gm_ws, gm_bs, w_in_b, sb_bias, w_mem_kv, mem_q_norm, mem_k_norm, w_out, w_up, conv_w, conv_b, w_down):
    depth = norm_mix.shape[0]
    bp, tp, d = x_prompt.shape
    bs_, ts, _ = x_sample.shape
    n_p, n_s = bp * tp, bs_ * ts

    k_gain_col = jnp.tile(mem_k_norm, (1, MEM_HEADS)).reshape(depth, MEM_WIDTH, 1)
    mem_kt_p, mem_vt_p = _mem_kv(mem_prompt, norm_mem, jnp.swapaxes(w_mem_kv, 1, 2).astype(BF16), k_gain_col)
    q_gain_full = jnp.tile(mem_q_norm, (1, MEM_HEADS))
    mem_kt_s = _feature_major(cache_mem_k)
    mem_vt_s = _feature_major(cache_mem_v)

    w_out_bf = w_out.astype(BF16)
    w_up_bf = w_up.astype(BF16)
    w_down_bf = w_down.astype(BF16)

    xp = x_prompt.reshape(n_p, d)
    xs = x_sample.reshape(n_s, d)
    sb_k_p, sb_v_p, sb_k_s, sb_v_s, gm_v_s, conv_p, conv_s = [], [], [], [], [], [], []
    for i in range(depth):
        if i % 2 == 0:
            a = i // 2
            w_bf = w_in_a[a].astype(BF16)
            bs_t = gm_bs[a].T
            mix_p, mq_p = _gmlp_in(xp, norm_mix[i], w_bf, gm_v_norm[a], gm_ws[a], bs_t,
                                   seq_len=CHUNK, cm=CHUNK, tm=PROJ_TILE, emit_v=False)
            mix_s, mq_s, v_new = _gmlp_in(xs, norm_mix[i], w_bf, gm_v_norm[a], gm_ws[a], bs_t,
                                          seq_len=ts, cm=n_s, tm=n_s, emit_v=True)
            gm_v_s.append(v_new.reshape(bs_, ts, MIX_WIDTH))
        else:
            b = i // 2
            w = w_in_b[b]
            wq_bf = jnp.concatenate([w[:, :MIX_WIDTH], w[:, 3 * MIX_WIDTH:]], axis=1).astype(BF16)
            wkvt_bf = w[:, MIX_WIDTH:3 * MIX_WIDTH].T.astype(BF16)
            q_p, kt_p, vt_p, ktb_p, vtb_p, mq_p = _sb_in_prompt(
                xp.reshape(bp, tp, d), norm_mix[i], wq_bf, wkvt_bf, tm=PROJ_TILE, tk=SB_TILE)
            mq_p = mq_p.reshape(n_p, MEM_WIDTH)
            q_s, k_s, v_s, mq_s = _sb_in_rows(xs, norm_mix[i], w.astype(BF16))
            mix_p = _sb_prompt(q_p, ktb_p, vtb_p, sb_bias[b], tq=SB_TILE).reshape(n_p, MIX_WIDTH)
            mix_s = _sb_decode(q_s.reshape(bs_, ts, MIX_WIDTH), k_s.reshape(bs_, ts, MIX_WIDTH),
                               v_s.reshape(bs_, ts, MIX_WIDTH),
                               _feature_major(cache_sb_k[b]), _feature_major(cache_sb_v[b]),
                               page_table, sb_bias[b]).reshape(n_s, MIX_WIDTH)
            sb_k_p.append(_position_major(kt_p, SB_HEADS))
            sb_v_p.append(_position_major(vt_p, SB_HEADS))
            sb_k_s.append(k_s.reshape(bs_, ts, SB_HEADS, SB_HEAD_DIM))
            sb_v_s.append(v_s.reshape(bs_, ts, SB_HEADS, SB_HEAD_DIM))
        xp3 = _mem_out(mix_p.reshape(bp, tp, MIX_WIDTH), mq_p.reshape(bp, tp, MEM_WIDTH), mem_kt_p, mem_vt_p,
                       q_gain_full[i], xp.reshape(bp, tp, d), w_out_bf, layer=i, tm=MEM_OUT_TILE)
        mem_out_s = _mem_attend(mq_s.reshape(bs_, ts, MEM_WIDTH), mem_kt_s, mem_vt_s, q_gain_full[i],
                                layer=i, seqs=MEM_SEQS_PER_STEP)
        xs = _out_proj(mix_s, mem_out_s.reshape(n_s, MEM_WIDTH), xs, w_out_bf, layer=i, tm=n_s)
        zero_rows = jnp.zeros((bp, CONV_W - 1, 2 * D_FF), F32)
        xp3, cp = _conv_ffn(xp3, norm_ffn[i], w_up_bf, conv_w[i], conv_b[i], w_down_bf,
                            zero_rows, layer=i, tm=FFN_TILE)
        xs, cs = _conv_ffn_rows(xs, norm_ffn[i], w_up_bf, conv_w[i], conv_b[i], w_down_bf,
                                state_ffn_conv[i], layer=i, seq_len=ts)
        xp = xp3.reshape(n_p, d)
        conv_p.append(cp)
        conv_s.append(cs)

    return (xp.reshape(bp, tp, d), xs.reshape(bs_, ts, d),
            _position_major(mem_kt_p, MEM_HEADS), _position_major(mem_vt_p, MEM_HEADS),
            jnp.stack(sb_k_p), jnp.stack(sb_v_p), jnp.stack(sb_k_s), jnp.stack(sb_v_s),
            jnp.stack(gm_v_s), jnp.stack(conv_p), jnp.stack(conv_s))
```
